```python
import math
import jax, jax.numpy as jnp
from jax import lax
import numpy as np

D_MODEL = 1024
BATCH = 16
SEQ = 2048
DEPTH = 4

CHUNK = 64
Q_BLOCK = 128
N_MEM = 256
N_A_LAYERS = DEPTH // 2
N_B_LAYERS = DEPTH - N_A_LAYERS
EPS = 1e-6
M_HEADS = 4
M_HEAD_DIM = D_MODEL // M_HEADS
M_WIDTH = M_HEADS * M_HEAD_DIM
CONV_W = 4
B_HEADS = 8
QK_NOPE = 128
QK_ROPE = 64
QK_HEAD = QK_NOPE + QK_ROPE
V_HEAD = 128
B_WIDTH = B_HEADS * V_HEAD
Q_LORA = 384
KV_LORA = 256
ROPE_THETA = 10000.0
MEM_HEADS = 4
MEM_HEAD_DIM = 128
MEM_WIDTH = MEM_HEADS * MEM_HEAD_DIM
A_IN = 2 * M_WIDTH + 3 * M_WIDTH + 2 * M_HEADS + 2 * MEM_WIDTH
B_IN = Q_LORA + B_WIDTH + 2 * MEM_WIDTH
A_MIX = M_WIDTH + MEM_WIDTH
B_MIX = B_WIDTH + MEM_WIDTH

kernel_name = "yoco_mlstm_mla_memory_trunk"


def rmsnorm(t, g):
    tf = t.astype(jnp.float32)
    y = tf * lax.rsqrt(jnp.mean(tf * tf, axis=-1, keepdims=True) + EPS)
    return (y * g.astype(jnp.float32)).astype(t.dtype)


def split_cols(t, sizes):
    offs = np.cumsum(np.array(sizes))[:-1].tolist()
    return jnp.split(t, offs, axis=-1)


def rope(t, positions):
    half = QK_ROPE // 2
    inv = ROPE_THETA ** (-jnp.arange(0, QK_ROPE, 2, dtype=jnp.float32) / QK_ROPE)
    ang = positions.astype(jnp.float32)[..., None] * inv
    cos = jnp.cos(ang)[:, :, None, :]
    sin = jnp.sin(ang)[:, :, None, :]
    tf = t.astype(jnp.float32)
    t1, t2 = tf[..., :half], tf[..., half:]
    return jnp.concatenate([t1 * cos - t2 * sin, t1 * sin + t2 * cos], axis=-1).astype(t.dtype)


def causal_conv(u, w, b):
    out = lax.conv_general_dilated(
        u, w[:, None, :].astype(u.dtype), window_strides=(1,), padding=[(CONV_W - 1, 0)],
        dimension_numbers=("NWC", "WIO", "NWC"), feature_group_count=u.shape[-1])
    return out + b.astype(u.dtype)


def mlstm_chunkwise(q, k, v, i_pre, f_pre):
    bsz, s, h, dh = q.shape
    nc = s // CHUNK

    def to_chunks(t):
        return t.astype(jnp.float32).reshape(bsz, nc, CHUNK, h, -1).transpose(1, 0, 3, 2, 4)

    def gate_chunks(t):
        return t.astype(jnp.float32).reshape(bsz, nc, CHUNK, h).transpose(1, 0, 3, 2)

    qc = to_chunks(q) * (dh ** -0.5)
    kc = to_chunks(k)
    vc = to_chunks(v)
    ic = gate_chunks(i_pre)
    lfc = jax.nn.log_sigmoid(gate_chunks(f_pre))
    causal = jnp.tril(jnp.ones((CHUNK, CHUNK), dtype=bool))

    def step(carry, inp):
        c_mat, n_vec, m = carry
        qb, kb, vb, ib, fb = inp
        b = jnp.cumsum(fb, axis=-1)
        d = b[..., :, None] - b[..., None, :] + ib[..., None, :]
        d = jnp.where(causal, d, -jnp.inf)
        inter = b + m[..., None]
        m_t = jnp.maximum(inter, jnp.max(d, axis=-1))
        p = jnp.einsum("bhtd,bhsd->bhts", qb, kb) * jnp.exp(d - m_t[..., None])
        g = jnp.exp(inter - m_t)
        num = jnp.einsum("bhts,bhsv->bhtv", p, vb) + g[..., None] * jnp.einsum("bhvk,bhtk->bhtv", c_mat, qb)
        den = jnp.sum(p, axis=-1) + g * jnp.einsum("bhk,bhtk->bht", n_vec, qb)
        h_out = num / jnp.maximum(jnp.abs(den), jnp.exp(-m_t))[..., None]
        b_last = b[..., -1]
        a = b_last[..., None] - b + ib
        m_new = jnp.maximum(b_last + m, jnp.max(a, axis=-1))
        wa = jnp.exp(a - m_new[..., None])
        gs = jnp.exp(b_last + m - m_new)
        c_new = gs[..., None, None] * c_mat + jnp.einsum("bhs,bhsv,bhsk->bhvk", wa, vb, kb)
        n_new = gs[..., None] * n_vec + jnp.einsum("bhs,bhsk->bhk", wa, kb)
        return (c_new, n_new, m_new), h_out

    init = (jnp.zeros((bsz, h, dh, dh), jnp.float32), jnp.zeros((bsz, h, dh), jnp.float32),
            jnp.zeros((bsz, h), jnp.float32))
    _, hs = lax.scan(step, init, (qc, kc, vc, ic, lfc))
    return hs.transpose(1, 0, 3, 2, 4).reshape(bsz, s, h, dh)


def chunk_causal_attention(q, k, v):
    s = q.shape[1]
    scale = QK_HEAD ** -0.5
    outs = []
    for j in range(s // Q_BLOCK):
        qs = j * Q_BLOCK
        ke = qs + Q_BLOCK
        sc = jnp.einsum("bqhd,bkhd->bhqk", q[:, qs:ke], k[:, :ke],
                        preferred_element_type=jnp.float32) * scale
        q_chunk = (qs + jnp.arange(Q_BLOCK)) // CHUNK
        k_chunk = jnp.arange(ke) // CHUNK
        sc = jnp.where(k_chunk[None, :] <= q_chunk[:, None], sc, -jnp.inf)
        p = jax.nn.softmax(sc, axis=-1).astype(v.dtype)
        outs.append(jnp.einsum("bhqk,bkhd->bqhd", p, v[:, :ke]))
    return jnp.concatenate(outs, axis=1)


def memory_kv(mem_n, w_kv, k_gain):
    bsz, nm, _ = mem_n.shape
    mk, mv = split_cols(mem_n @ w_kv, [MEM_WIDTH, MEM_WIDTH])
    mk = rmsnorm(mk.reshape(bsz, nm, MEM_HEADS, MEM_HEAD_DIM), k_gain)
    return mk, mv.reshape(bsz, nm, MEM_HEADS, MEM_HEAD_DIM)


def memory_attention(mq, mz, mk, mv, q_gain):
    bsz, s, _ = mq.shape
    q = rmsnorm(mq.reshape(bsz, s, MEM_HEADS, MEM_HEAD_DIM), q_gain)
    sc = jnp.einsum("bqhd,bkhd->bhqk", q, mk, preferred_element_type=jnp.float32) * (MEM_HEAD_DIM ** -0.5)
    p = jax.nn.softmax(sc, axis=-1).astype(mv.dtype)
    out = jnp.einsum("bhqk,bkhd->bqhd", p, mv).reshape(bsz, s, MEM_WIDTH)
    return out * jax.nn.silu(mz)


def mlstm_layer(x, norm_g, w_in, conv_w, conv_b, ig_b, fg_b, h_g, w_out, mk, mv, mq_gain):
    bsz, s, _ = x.shape
    h = rmsnorm(x, norm_g)
    qk, v, o, z, i_pre, f_pre, mq, mz = split_cols(
        h @ w_in, [2 * M_WIDTH, M_WIDTH, M_WIDTH, M_WIDTH, M_HEADS, M_HEADS, MEM_WIDTH, MEM_WIDTH])
    qk = jax.nn.silu(causal_conv(qk, conv_w, conv_b))
    q, k = split_cols(qk, [M_WIDTH, M_WIDTH])
    heads = (bsz, s, M_HEADS, M_HEAD_DIM)
    ht = mlstm_chunkwise(q.reshape(heads), k.reshape(heads), v.reshape(heads),
                         i_pre + ig_b, f_pre + fg_b).astype(x.dtype)
    ht = rmsnorm(ht, h_g.reshape(M_HEADS, M_HEAD_DIM)).reshape(bsz, s, M_WIDTH)
    y_m = jax.nn.sigmoid(o) * ht * jax.nn.silu(z)
    y_mem = memory_attention(mq, mz, mk, mv, mq_gain)
    return x + jnp.concatenate([y_m, y_mem], axis=-1) @ w_out


def shared_kv(x, positions, kv_norm, w_kv_a, kv_lat_norm, w_kv_b, k_gain):
    bsz, s, _ = x.shape
    h = rmsnorm(x, kv_norm)
    c_kv, k_pe = split_cols(h @ w_kv_a, [KV_LORA, QK_ROPE])
    kv = (rmsnorm(c_kv, kv_lat_norm) @ w_kv_b).reshape(bsz, s, B_HEADS, QK_NOPE + V_HEAD)
    k_nope, v = kv[..., :QK_NOPE], kv[..., QK_NOPE:]
    k = jnp.concatenate([k_nope, jnp.broadcast_to(k_pe[:, :, None, :], (bsz, s, B_HEADS, QK_ROPE))], axis=-1)
    k = rmsnorm(k, k_gain)
    k = jnp.concatenate([k[..., :QK_NOPE], rope(k[..., QK_NOPE:], positions)], axis=-1)
    return k, v


def mla_layer(x, positions, k_sh, v_sh, norm_g, w_in, q_lat_g, w_q_up, q_gain, w_out, mk, mv, mq_gain):
    bsz, s, _ = x.shape
    h = rmsnorm(x, norm_g)
    q_lat, z, mq, mz = split_cols(h @ w_in, [Q_LORA, B_WIDTH, MEM_WIDTH, MEM_WIDTH])
    q = (rmsnorm(q_lat, q_lat_g) @ w_q_up).reshape(bsz, s, B_HEADS, QK_HEAD)
    q = rmsnorm(q, q_gain)
    q = jnp.concatenate([q[..., :QK_NOPE], rope(q[..., QK_NOPE:], positions)], axis=-1)
    attn = chunk_causal_attention(q, k_sh, v_sh).reshape(bsz, s, B_WIDTH)
    y_b = attn * jax.nn.silu(z)
    y_mem = memory_attention(mq, mz, mk, mv, mq_gain)
    return x + jnp.concatenate([y_b, y_mem], axis=-1) @ w_out


def setup_inputs(seed: int = 0) -> dict:
    key = jax.random.key(seed)
    ks = jax.random.split(key, 32)

    def nrm(k, shape, scale):
        return jax.random.normal(k, shape, jnp.float32) * scale

    def gain(k, shape):
        return 1.0 + 0.02 * jax.random.normal(k, shape, jnp.float32)

    offsets = jax.random.randint(ks[2], (BATCH,), 0, 4096, dtype=jnp.int32)
    positions = (offsets[:, None] + jnp.arange(SEQ, dtype=jnp.int32)[None, :]).astype(jnp.int32)
    fg_base = jnp.linspace(3.0, 6.0, M_HEADS, dtype=jnp.float32)
    return {
        "x": nrm(ks[0], (BATCH, SEQ, D_MODEL), 1.0),
        "mem": nrm(ks[1], (BATCH, N_MEM, D_MODEL), 1.0),
        "positions": positions,
        "a_norm": gain(ks[3], (N_A_LAYERS, D_MODEL)),
        "a_w_in": nrm(ks[4], (N_A_LAYERS, D_MODEL, A_IN), D_MODEL ** -0.5),
        "a_conv_w": nrm(ks[5], (N_A_LAYERS, CONV_W, 2 * M_WIDTH), CONV_W ** -0.5),
        "a_conv_b": nrm(ks[6], (N_A_LAYERS, 2 * M_WIDTH), 0.01),
        "a_ig_bias": nrm(ks[7], (N_A_LAYERS, M_HEADS), 0.1),
        "a_fg_bias": fg_base[None, :] + nrm(ks[8], (N_A_LAYERS, M_HEADS), 0.1),
        "a_h_norm": gain(ks[9], (N_A_LAYERS, M_WIDTH)),
        "a_w_out": nrm(ks[10], (N_A_LAYERS, A_MIX, D_MODEL), A_MIX ** -0.5),
        "b_norm": gain(ks[11], (N_B_LAYERS, D_MODEL)),
        "b_w_in": nrm(ks[12], (N_B_LAYERS, D_MODEL, B_IN), D_MODEL ** -0.5),
        "b_q_lat_norm": gain(ks[13], (N_B_LAYERS, Q_LORA)),
        "b_w_q_up": nrm(ks[14], (N_B_LAYERS, Q_LORA, B_HEADS * QK_HEAD), Q_LORA ** -0.5),
        "b_q_gain": gain(ks[15], (N_B_LAYERS, QK_HEAD)),
        "b_w_out": nrm(ks[16], (N_B_LAYERS, B_MIX, D_MODEL), B_MIX ** -0.5),
        "kv_norm": gain(ks[17], (D_MODEL,)),
        "w_kv_a": nrm(ks[18], (D_MODEL, KV_LORA + QK_ROPE), D_MODEL ** -0.5),
        "kv_lat_norm": gain(ks[19], (KV_LORA,)),
        "w_kv_b": nrm(ks[20], (KV_LORA, B_HEADS * (QK_NOPE + V_HEAD)), KV_LORA ** -0.5),
        "k_gain": gain(ks[21], (QK_HEAD,)),
        "mem_norm": gain(ks[22], (D_MODEL,)),
        "mem_w_kv": nrm(ks[23], (DEPTH, D_MODEL, 2 * MEM_WIDTH), D_MODEL ** -0.5),
        "mem_q_gain": gain(ks[24], (DEPTH, MEM_HEAD_DIM)),
        "mem_k_gain": gain(ks[25], (DEPTH, MEM_HEAD_DIM)),
    }


def reference(x, mem, positions, a_norm, a_w_in, a_conv_w, a_conv_b, a_ig_bias, a_fg_bias, a_h_norm, a_w_out,
              b_norm, b_w_in, b_q_lat_norm, b_w_q_up, b_q_gain, b_w_out,
              kv_norm, w_kv_a, kv_lat_norm, w_kv_b, k_gain,
              mem_norm, mem_w_kv, mem_q_gain, mem_k_gain):
    mem_n = rmsnorm(mem, mem_norm)
    k_sh = None
    v_sh = None
    for layer in range(DEPTH):
        mk, mv = memory_kv(mem_n, mem_w_kv[layer], mem_k_gain[layer])
        if layer < N_A_LAYERS:
            x = mlstm_layer(x, a_norm[layer], a_w_in[layer], a_conv_w[layer], a_conv_b[layer],
                            a_ig_bias[layer], a_fg_bias[layer], a_h_norm[layer], a_w_out[layer],
                            mk, mv, mem_q_gain[layer])
        else:
            if layer == N_A_LAYERS:
                k_sh, v_sh = shared_kv(x, positions, kv_norm, w_kv_a, kv_lat_norm, w_kv_b, k_gain)
            j = layer - N_A_LAYERS
            x = mla_layer(x, positions, k_sh, v_sh, b_norm[j], b_w_in[j], b_q_lat_norm[j], b_w_q_up[j],
                          b_q_gain[j], b_w_out[j], mk, mv, mem_q_gain[layer])
    return x
```

```python
import functools

import jax
import jax.numpy as jnp
import numpy as np
from jax import lax
from jax.experimental import pallas as pl
from jax.experimental.pallas import tpu as pltpu

F32 = jnp.float32
BF16 = jnp.bfloat16

D_MODEL = 1024
DEPTH = 4
N_A = DEPTH // 2
CHUNK = 64
EPS = 1e-6
M_HEADS = 4
M_HEAD_DIM = 256
M_WIDTH = 1024
CONV_W = 4
B_HEADS = 8
QK_NOPE = 128
QK_ROPE = 64
QK_HEAD = 192
QK_PAD = 256
V_HEAD = 128
B_WIDTH = 1024
Q_LORA = 384
KV_LORA = 256
ROPE_THETA = 10000.0
MEM_HEADS = 4
MEM_HEAD_DIM = 128
MEM_WIDTH = 512
A_MAIN = 5 * M_WIDTH
LANES = 128

VMEM_LIMIT = 56 * 1024 * 1024

TM = 256
MLSTM_TS = 256
MLSTM_L = 128
ATTN_TQ = 256

_NT = (((1,), (1,)), ((), ()))


def _params(sem):
    return pltpu.CompilerParams(dimension_semantics=sem, vmem_limit_bytes=VMEM_LIMIT)


def _rms(t, g, n):
    ms = jnp.sum(t * t, axis=-1, keepdims=True) * (1.0 / n)
    return t * lax.rsqrt(ms + EPS) * g


def _silu(t):
    return t * jax.nn.sigmoid(t)


def _dot(a, b):
    return jnp.dot(a, b, preferred_element_type=F32)


def _memkv_kernel(mem_ref, g_ref, w_ref, kg_ref, mk_ref, mv_ref):
    mn = _rms(mem_ref[0], g_ref[...], D_MODEL).astype(BF16)
    kv = _dot(mn, w_ref[0])
    for h in range(MEM_HEADS):
        sl = slice(h * MEM_HEAD_DIM, (h + 1) * MEM_HEAD_DIM)
        mk_ref[0, 0, :, sl] = _rms(kv[:, sl], kg_ref[0], MEM_HEAD_DIM).astype(BF16)
    mv_ref[0, 0] = kv[:, MEM_WIDTH:].astype(BF16)


def _memory_kv(mem, mem_norm, w_kv, k_gain):
    bsz, nm, _ = mem.shape
    out = jax.ShapeDtypeStruct((DEPTH, bsz, nm, MEM_WIDTH), BF16)
    return pl.pallas_call(
        _memkv_kernel,
        grid=(DEPTH, bsz),
        in_specs=[
            pl.BlockSpec((1, nm, D_MODEL), lambda l, b: (b, 0, 0)),
            pl.BlockSpec((1, D_MODEL), lambda l, b: (0, 0)),
            pl.BlockSpec((1, D_MODEL, 2 * MEM_WIDTH), lambda l, b: (l, 0, 0)),
            pl.BlockSpec((1, 1, MEM_HEAD_DIM), lambda l, b: (l, 0, 0)),
        ],
        out_specs=[
            pl.BlockSpec((1, 1, nm, MEM_WIDTH), lambda l, b: (l, b, 0, 0)),
            pl.BlockSpec((1, 1, nm, MEM_WIDTH), lambda l, b: (l, b, 0, 0)),
        ],
        out_shape=[out, out],
        compiler_params=_params(("arbitrary", "arbitrary")),
        name="memory_kv",
    )(mem, mem_norm.reshape(1, D_MODEL), w_kv.astype(BF16), k_gain.reshape(DEPTH, 1, MEM_HEAD_DIM))


def _inproj_a_kernel(x_ref, g_ref, wmain_ref, wgate_ref, wmem_ref, main_ref, gate_ref, mm_ref):
    h = _rms(x_ref[...], g_ref[...], D_MODEL).astype(BF16)
    for c in range(A_MAIN // M_WIDTH):
        sl = slice(c * M_WIDTH, (c + 1) * M_WIDTH)
        main_ref[:, sl] = _dot(h, wmain_ref[:, sl]).astype(BF16)
    gate_ref[...] = _dot(h, wgate_ref[...])
    mm_ref[...] = _dot(h, wmem_ref[...]).astype(BF16)


def _inproj_a(x2d, norm_g, w_in):
    t = x2d.shape[0]
    wmain = w_in[:, :A_MAIN].astype(BF16)
    wgate = jnp.pad(w_in[:, A_MAIN:A_MAIN + 2 * M_HEADS], ((0, 0), (0, LANES - 2 * M_HEADS))).astype(BF16)
    wmem = w_in[:, A_MAIN + 2 * M_HEADS:].astype(BF16)
    const = lambda i: (0, 0)
    return pl.pallas_call(
        _inproj_a_kernel,
        grid=(t // TM,),
        in_specs=[
            pl.BlockSpec((TM, D_MODEL), lambda i: (i, 0)),
            pl.BlockSpec((1, D_MODEL), const),
            pl.BlockSpec((D_MODEL, A_MAIN), const),
            pl.BlockSpec((D_MODEL, LANES), const),
            pl.BlockSpec((D_MODEL, 2 * MEM_WIDTH), const),
        ],
        out_specs=[
            pl.BlockSpec((TM, A_MAIN), lambda i: (i, 0)),
            pl.BlockSpec((TM, LANES), lambda i: (i, 0)),
            pl.BlockSpec((TM, 2 * MEM_WIDTH), lambda i: (i, 0)),
        ],
        out_shape=[
            jax.ShapeDtypeStruct((t, A_MAIN), BF16),
            jax.ShapeDtypeStruct((t, LANES), F32),
            jax.ShapeDtypeStruct((t, 2 * MEM_WIDTH), BF16),
        ],
        compiler_params=_params(("arbitrary",)),
        name="inproj_a",
    )(x2d, norm_g.reshape(1, D_MODEL), wmain, wgate, wmem)


def _mlstm_kernel(main_ref, gate_ref, gbias_ref, cw_ref, cb_ref, hg_ref, y_ref,
                  ext_ref, ct_ref, n_ref, m_ref):
    ts, ln, dh = MLSTM_TS, MLSTM_L, M_HEAD_DIM

    @pl.when(pl.program_id(1) == 0)
    def _():
        ext_ref[:, 0:8, :] = jnp.zeros((2 * M_HEADS, 8, dh), F32)
        ct_ref[...] = jnp.zeros_like(ct_ref)
        n_ref[...] = jnp.zeros_like(n_ref)
        m_ref[...] = jnp.zeros_like(m_ref)

    def conv_act(col0, slot):
        cs = slice(col0, col0 + dh)
        u = main_ref[0, :, cs].astype(F32)
        ext_ref[slot, 8:8 + ts, :] = u
        acc = cb_ref[:, cs] + cw_ref[CONV_W - 1:CONV_W, cs] * u
        for j in range(CONV_W - 1):
            lo = 8 - (CONV_W - 1) + j
            acc = acc + cw_ref[j:j + 1, cs] * ext_ref[slot, lo:lo + ts, :]
        ext_ref[slot, 0:8, :] = u[ts - 8:ts, :]
        return _silu(acc)

    g = gate_ref[0] + gbias_ref[...]
    lsg = jnp.minimum(g, 0.0) - jnp.log1p(jnp.exp(-jnp.abs(g)))
    r_i = lax.broadcasted_iota(jnp.int32, (ts, ts), 0)
    c_i = lax.broadcasted_iota(jnp.int32, (ts, ts), 1)
    sh = ln.bit_length() - 1
    seg = jnp.where((r_i >= c_i) & ((r_i >> sh) == (c_i >> sh)), 1.0, 0.0).astype(F32)
    bcum = jnp.dot(seg, lsg, precision=lax.Precision.HIGHEST,
                   preferred_element_type=F32)
    g_t = g.T
    b_t = bcum.T
    tri = (lax.broadcasted_iota(jnp.int32, (ln, ln), 0)
           >= lax.broadcasted_iota(jnp.int32, (ln, ln), 1))

    for h in range(M_HEADS):
        qa = conv_act(h * dh, h) * (dh ** -0.5)
        ka = conv_act(M_WIDTH + h * dh, M_HEADS + h)
        hs = slice(h * dh, (h + 1) * dh)
        for c in range(ts // ln):
            rs = slice(c * ln, (c + 1) * ln)
            qf = qa[rs]
            kf = ka[rs]
            qb = qf.astype(BF16)
            vb = main_ref[0, rs, 2 * M_WIDTH + h * dh:2 * M_WIDTH + (h + 1) * dh]
            b_col = bcum[rs, M_HEADS + h:M_HEADS + h + 1]
            b_row = b_t[M_HEADS + h:M_HEADS + h + 1, rs]
            i_col = g[rs, h:h + 1]
            i_row = g_t[h:h + 1, rs]
            m_prev = m_ref[h]
            ct = ct_ref[h]
            n_row = n_ref[h]

            d = jnp.where(tri, b_col - b_row + i_row, -jnp.inf)
            inter = b_col + m_prev
            m_t = jnp.maximum(inter, jnp.max(d, axis=1, keepdims=True))
            s = lax.dot_general(qb, kf.astype(BF16), _NT, preferred_element_type=F32) * jnp.exp(d - m_t)
            g_int = jnp.exp(inter - m_t)
            num = _dot(s.astype(BF16), vb) + g_int * _dot(qb, ct.astype(BF16))
            den = (jnp.sum(s, axis=1, keepdims=True)
                   + g_int * jnp.sum(qf * n_row, axis=1, keepdims=True))
            hout = num / jnp.maximum(jnp.abs(den), jnp.exp(-m_t))

            b_last = b_col[ln - 1:ln, :]
            a_col = b_last - b_col + i_col
            m_new = jnp.maximum(b_last + m_prev, jnp.max(a_col, axis=0, keepdims=True))
            kw = kf * jnp.exp(a_col - m_new)
            gs = jnp.exp(b_last + m_prev - m_new)
            ct_ref[h] = gs * ct + _dot(kw.T.astype(BF16), vb)
            n_ref[h] = gs * n_row + jnp.sum(kw, axis=0, keepdims=True)
            m_ref[h] = m_new

            ht = _rms(hout, hg_ref[:, hs], dh)
            o = main_ref[0, rs, 3 * M_WIDTH + h * dh:3 * M_WIDTH + (h + 1) * dh].astype(F32)
            z = main_ref[0, rs, 4 * M_WIDTH + h * dh:4 * M_WIDTH + (h + 1) * dh].astype(F32)
            y_ref[0, rs, hs] = (jax.nn.sigmoid(o) * ht * _silu(z)).astype(BF16)


def _mlstm(main, gates, conv_w, conv_b, ig_b, fg_b, h_g):
    bsz, s, _ = main.shape
    ts = MLSTM_TS
    gbias = jnp.concatenate([ig_b, fg_b, jnp.zeros((LANES - 2 * M_HEADS,), F32)]).reshape(1, LANES)
    const = lambda b, t: (0, 0)
    return pl.pallas_call(
        _mlstm_kernel,
        grid=(bsz, s // ts),
        in_specs=[
            pl.BlockSpec((1, ts, A_MAIN), lambda b, t: (b, t, 0)),
            pl.BlockSpec((1, ts, LANES), lambda b, t: (b, t, 0)),
            pl.BlockSpec((1, LANES), const),
            pl.BlockSpec((CONV_W, 2 * M_WIDTH), const),
            pl.BlockSpec((1, 2 * M_WIDTH), const),
            pl.BlockSpec((1, M_WIDTH), const),
        ],
        out_specs=pl.BlockSpec((1, ts, M_WIDTH), lambda b, t: (b, t, 0)),
        out_shape=jax.ShapeDtypeStruct((bsz, s, M_WIDTH), BF16),
        scratch_shapes=[
            pltpu.VMEM((2 * M_HEADS, 8 + ts, M_HEAD_DIM), F32),
            pltpu.VMEM((M_HEADS, M_HEAD_DIM, M_HEAD_DIM), F32),
            pltpu.VMEM((M_HEADS, 1, M_HEAD_DIM), F32),
            pltpu.VMEM((M_HEADS, 1, 1), F32),
        ],
        compiler_params=_params(("arbitrary", "arbitrary")),
        name="mlstm",
    )(main, gates, gbias, conv_w, conv_b.reshape(1, 2 * M_WIDTH), h_g.reshape(1, M_WIDTH))


def _outproj_kernel(y_ref, mm_ref, mk_ref, mv_ref, qg_ref, w_ref, x_ref, o_ref):
    acc = x_ref[0] + _dot(y_ref[0], w_ref[0:M_WIDTH, :])
    ymem = []
    for h in range(MEM_HEADS):
        sl = slice(h * MEM_HEAD_DIM, (h + 1) * MEM_HEAD_DIM)
        q = _rms(mm_ref[0, :, sl].astype(F32), qg_ref[...], MEM_HEAD_DIM) * (MEM_HEAD_DIM ** -0.5)
        sc = lax.dot_general(q.astype(BF16), mk_ref[0, 0, :, sl], _NT, preferred_element_type=F32)
        p = jnp.exp(sc - jnp.max(sc, axis=-1, keepdims=True))
        att = _dot(p.astype(BF16), mv_ref[0, 0, :, sl]) / jnp.sum(p, axis=-1, keepdims=True)
        z = mm_ref[0, :, MEM_WIDTH + h * MEM_HEAD_DIM:MEM_WIDTH + (h + 1) * MEM_HEAD_DIM].astype(F32)
        ymem.append((att * _silu(z)).astype(BF16))
    o_ref[0] = acc + _dot(jnp.concatenate(ymem, axis=-1), w_ref[M_WIDTH:, :])


def _outproj(y, mm, mk_all, mv_all, layer, q_gain, w_out, x):
    bsz, s, _ = x.shape
    nm = mk_all.shape[2]
    kv_spec = pl.BlockSpec((1, 1, nm, MEM_WIDTH), lambda b, t: (layer, b, 0, 0))
    tile = lambda w: pl.BlockSpec((1, TM, w), lambda b, t: (b, t, 0))
    return pl.pallas_call(
        _outproj_kernel,
        grid=(bsz, s // TM),
        in_specs=[
            tile(M_WIDTH), tile(2 * MEM_WIDTH), kv_spec, kv_spec,
            pl.BlockSpec((1, MEM_HEAD_DIM), lambda b, t: (0, 0)),
            pl.BlockSpec((M_WIDTH + MEM_WIDTH, D_MODEL), lambda b, t: (0, 0)),
            tile(D_MODEL),
        ],
        out_specs=tile(D_MODEL),
        out_shape=jax.ShapeDtypeStruct((bsz, s, D_MODEL), F32),
        compiler_params=_params(("arbitrary", "arbitrary")),
        name="outproj",
    )(y, mm, mk_all, mv_all, q_gain.reshape(1, MEM_HEAD_DIM), w_out.astype(BF16), x)


def _rope_tables(pos_ref, tab_ref):
    ang = pos_ref[0].astype(F32) * tab_ref[0:1, :]
    return jnp.cos(ang) * tab_ref[1:2, :], jnp.sin(ang) * tab_ref[2:3, :]


def _norm_rope(t, gain, cos_t, sin_t):
    half = QK_ROPE // 2
    t = _rms(t, gain, QK_HEAD)
    r = t[:, QK_NOPE:]
    rot = pltpu.roll(r, half, 1) + pltpu.roll(r, LANES - half, 1)
    return t[:, :QK_NOPE], r * cos_t + rot * sin_t


def _inproj_b_kernel(x_ref, pos_ref, tab_ref, g_ref, win_ref, qlg_ref, wq_ref, qg_ref,
                     q_ref, zz_ref, mm_ref):
    h = _rms(x_ref[0], g_ref[...], D_MODEL).astype(BF16)
    zz_ref[0] = _dot(h, win_ref[:, Q_LORA:Q_LORA + B_WIDTH]).astype(BF16)
    mm_ref[0] = _dot(h, win_ref[:, Q_LORA + B_WIDTH:]).astype(BF16)
    ql = _rms(_dot(h, win_ref[:, 0:Q_LORA]), qlg_ref[...], Q_LORA).astype(BF16)
    cos_t, sin_t = _rope_tables(pos_ref, tab_ref)
    scale = QK_HEAD ** -0.5
    for hd in range(B_HEADS):
        t = _dot(ql, wq_ref[:, hd * QK_PAD:(hd + 1) * QK_PAD])
        nope, rot = _norm_rope(t, qg_ref[...], cos_t, sin_t)
        q_ref[0, hd, :, 0:QK_NOPE] = (nope * scale).astype(BF16)
        q_ref[0, hd, :, QK_NOPE:] = (rot * scale).astype(BF16)


def _pad_heads(w, heads, width):
    lead = w.shape[:-1]
    w = w.reshape(lead + (heads, width))
    w = jnp.pad(w, [(0, 0)] * len(lead) + [(0, 0), (0, QK_PAD - width)])
    return w.reshape(lead + (heads * QK_PAD,))


def _rope_const():
    half = QK_ROPE // 2
    inv = ROPE_THETA ** (-jnp.arange(0, QK_ROPE, 2, dtype=F32) / QK_ROPE)
    zeros = jnp.zeros((LANES - QK_ROPE,), F32)
    ones = jnp.ones((half,), F32)
    rows = [jnp.concatenate([inv, inv, zeros]),
            jnp.concatenate([ones, ones, zeros]),
            jnp.concatenate([-ones, ones, zeros])]
    return jnp.concatenate([jnp.stack(rows), jnp.zeros((5, LANES), F32)], axis=0)


def _inproj_b(x, pos3, norm_g, w_in, q_lat_g, w_q_up, q_gain):
    bsz, s, _ = x.shape
    b_in = w_in.shape[1]
    const = lambda b, t: (0, 0)
    tile = lambda w: pl.BlockSpec((1, TM, w), lambda b, t: (b, t, 0))
    return pl.pallas_call(
        _inproj_b_kernel,
        grid=(bsz, s // TM),
        in_specs=[
            tile(D_MODEL), tile(1),
            pl.BlockSpec((8, LANES), const),
            pl.BlockSpec((1, D_MODEL), const),
            pl.BlockSpec((D_MODEL, b_in), const),
            pl.BlockSpec((1, Q_LORA), const),
            pl.BlockSpec((Q_LORA, B_HEADS * QK_PAD), const),
            pl.BlockSpec((1, QK_PAD), const),
        ],
        out_specs=[
            pl.BlockSpec((1, B_HEADS, TM, QK_PAD), lambda b, t: (b, 0, t, 0)),
            tile(B_WIDTH), tile(2 * MEM_WIDTH),
        ],
        out_shape=[
            jax.ShapeDtypeStruct((bsz, B_HEADS, s, QK_PAD), BF16),
            jax.ShapeDtypeStruct((bsz, s, B_WIDTH), BF16),
            jax.ShapeDtypeStruct((bsz, s, 2 * MEM_WIDTH), BF16),
        ],
        compiler_params=_params(("arbitrary", "arbitrary")),
        name="inproj_b",
    )(x, pos3, _rope_const(), norm_g.reshape(1, D_MODEL), w_in.astype(BF16),
      q_lat_g.reshape(1, Q_LORA), _pad_heads(w_q_up, B_HEADS, QK_HEAD).astype(BF16),
      jnp.pad(q_gain, (0, QK_PAD - QK_HEAD)).reshape(1, QK_PAD))


def _sharedkv_kernel(x_ref, pos_ref, tab_ref, g_ref, wa_ref, lg_ref, wb_ref, kg_ref, k_ref, v_ref):
    h = _rms(x_ref[0], g_ref[...], D_MODEL).astype(BF16)
    a = _dot(h, wa_ref[...])
    c = _rms(a[:, 0:KV_LORA], lg_ref[...], KV_LORA).astype(BF16)
    k_pe = a[:, KV_LORA:]
    cos_t, sin_t = _rope_tables(pos_ref, tab_ref)
    for hd in range(B_HEADS):
        kv = _dot(c, wb_ref[:, hd * (QK_NOPE + V_HEAD):(hd + 1) * (QK_NOPE + V_HEAD)])
        t = jnp.concatenate([kv[:, 0:QK_NOPE], k_pe], axis=-1)
        nope, rot = _norm_rope(t, kg_ref[...], cos_t, sin_t)
        k_ref[0, hd, :, 0:QK_NOPE] = nope.astype(BF16)
        k_ref[0, hd, :, QK_NOPE:] = rot.astype(BF16)
        v_ref[0, hd] = kv[:, QK_NOPE:].astype(BF16)


def _shared_kv(x, pos3, kv_norm, w_kv_a, kv_lat_norm, w_kv_b, k_gain):
    bsz, s, _ = x.shape
    a_w = KV_LORA + LANES
    const = lambda b, t: (0, 0)
    tile = lambda w: pl.BlockSpec((1, TM, w), lambda b, t: (b, t, 0))
    return pl.pallas_call(
        _sharedkv_kernel,
        grid=(bsz, s // TM),
        in_specs=[
            tile(D_MODEL), tile(1),
            pl.BlockSpec((8, LANES), const),
            pl.BlockSpec((1, D_MODEL), const),
            pl.BlockSpec((D_MODEL, a_w), const),
            pl.BlockSpec((1, KV_LORA), const),
            pl.BlockSpec((KV_LORA, B_HEADS * (QK_NOPE + V_HEAD)), const),
            pl.BlockSpec((1, QK_PAD), const),
        ],
        out_specs=[
            pl.BlockSpec((1, B_HEADS, TM, QK_PAD), lambda b, t: (b, 0, t, 0)),
            pl.BlockSpec((1, B_HEADS, TM, V_HEAD), lambda b, t: (b, 0, t, 0)),
        ],
        out_shape=[
            jax.ShapeDtypeStruct((bsz, B_HEADS, s, QK_PAD), BF16),
            jax.ShapeDtypeStruct((bsz, B_HEADS, s, V_HEAD), BF16),
        ],
        compiler_params=_params(("arbitrary", "arbitrary")),
        name="shared_kv",
    )(x, pos3, _rope_const(), kv_norm.reshape(1, D_MODEL),
      jnp.pad(w_kv_a, ((0, 0), (0, a_w - w_kv_a.shape[1]))).astype(BF16),
      kv_lat_norm.reshape(1, KV_LORA), w_kv_b.astype(BF16),
      jnp.pad(k_gain, (0, QK_PAD - QK_HEAD)).reshape(1, QK_PAD))


def _attn_kernel(q_ref, k_ref, v_ref, z_ref, o_ref):
    tq = ATTN_TQ
    s = q_ref.shape[2]
    sh = CHUNK.bit_length() - 1
    rc = lax.broadcasted_iota(jnp.int32, (tq, tq), 0) >> sh
    cc = lax.broadcasted_iota(jnp.int32, (tq, tq), 1) >> sh
    dmask = cc <= rc
    for qi in range(s // tq):
        rows = slice(qi * tq, (qi + 1) * tq)
        q = q_ref[0, 0, rows, :]
        sd = lax.dot_general(q, k_ref[0, 0, rows, :], _NT, preferred_element_type=F32)
        sd = jnp.where(dmask, sd, -jnp.inf)
        m = jnp.max(sd, axis=-1, keepdims=True)
        if qi > 0:
            past = slice(0, qi * tq)
            so = lax.dot_general(q, k_ref[0, 0, past, :], _NT, preferred_element_type=F32)
            m = jnp.maximum(m, jnp.max(so, axis=-1, keepdims=True))
            po = jnp.exp(so - m)
            l = jnp.sum(po, axis=-1, keepdims=True)
            acc = _dot(po.astype(BF16), v_ref[0, 0, past, :])
        pd = jnp.exp(sd - m)
        if qi > 0:
            l = l + jnp.sum(pd, axis=-1, keepdims=True)
            acc = acc + _dot(pd.astype(BF16), v_ref[0, 0, rows, :])
        else:
            l = jnp.sum(pd, axis=-1, keepdims=True)
            acc = _dot(pd.astype(BF16), v_ref[0, 0, rows, :])
        o_ref[0, rows, :] = (acc / l * _silu(z_ref[0, rows, :].astype(F32))).astype(BF16)


def _attention(q, k, v, zz):
    bsz, heads, s, _ = q.shape
    head_spec = lambda w: pl.BlockSpec((1, 1, s, w), lambda b, h: (b, h, 0, 0))
    col_spec = pl.BlockSpec((1, s, V_HEAD), lambda b, h: (b, 0, h))
    return pl.pallas_call(
        _attn_kernel,
        grid=(bsz, heads),
        in_specs=[head_spec(QK_PAD), head_spec(QK_PAD), head_spec(V_HEAD), col_spec],
        out_specs=col_spec,
        out_shape=jax.ShapeDtypeStruct((bsz, s, B_WIDTH), BF16),
        compiler_params=_params(("arbitrary", "arbitrary")),
        name="mla_attention",
    )(q, k, v, zz)


def kernel(x, mem, positions, a_norm, a_w_in, a_conv_w, a_conv_b, a_ig_bias, a_fg_bias, a_h_norm, a_w_out,
           b_norm, b_w_in, b_q_lat_norm, b_w_q_up, b_q_gain, b_w_out,
           kv_norm, w_kv_a, kv_lat_norm, w_kv_b, k_gain,
           mem_norm, mem_w_kv, mem_q_gain, mem_k_gain):
    bsz, s, _ = x.shape
    assert s % MLSTM_TS == 0 and s % TM == 0 and s % ATTN_TQ == 0 and MLSTM_TS % MLSTM_L == 0
    mk_all, mv_all = _memory_kv(mem, mem_norm, mem_w_kv, mem_k_gain)
    pos3 = positions.reshape(bsz, s, 1)
    k_sh = v_sh = None
    for layer in range(DEPTH):
        if layer < N_A:
            main, gates, mm = _inproj_a(x.reshape(bsz * s, D_MODEL), a_norm[layer], a_w_in[layer])
            y = _mlstm(main.reshape(bsz, s, A_MAIN), gates.reshape(bsz, s, LANES),
                       a_conv_w[layer], a_conv_b[layer], a_ig_bias[layer], a_fg_bias[layer], a_h_norm[layer])
            x = _outproj(y, mm.reshape(bsz, s, 2 * MEM_WIDTH), mk_all, mv_all, layer,
                         mem_q_gain[layer], a_w_out[layer], x)
        else:
            j = layer - N_A
            if k_sh is None:
                k_sh, v_sh = _shared_kv(x, pos3, kv_norm, w_kv_a, kv_lat_norm, w_kv_b, k_gain)
            q, zz, mm = _inproj_b(x, pos3, b_norm[j], b_w_in[j], b_q_lat_norm[j], b_w_q_up[j], b_q_gain[j])
            y = _attention(q, k_sh, v_sh, zz)
            x = _outproj(y, mm, mk_all, mv_all, layer, mem_q_gain[layer], b_w_out[j], x)
    return x
```

```python
import functools

import jax
import jax.numpy as jnp
from jax import lax
from jax.experimental import pallas as pl
from jax.experimental.pallas import tpu as pltpu

F32 = jnp.float32
BF16 = jnp.bfloat16

D_MODEL = 1024
DEPTH = 4
N_A = DEPTH // 2
CHUNK = 64
EPS = 1e-6
M_HEADS = 4
M_HEAD_DIM = 256
M_WIDTH = 1024
CONV_W = 4
B_HEADS = 8
QK_NOPE = 128
QK_ROPE = 64
QK_HEAD = 192
QK_PAD = 256
V_HEAD = 128
B_WIDTH = 1024
Q_LORA = 384
KV_LORA = 256
ROPE_THETA = 10000.0
MEM_HEADS = 4
MEM_HEAD_DIM = 128
MEM_WIDTH = 512
A_MAIN = 5 * M_WIDTH
A_MIX = 4 * M_WIDTH
LANES = 128
SUBLANES = 8
LOG2E = 1.4426950408889634

VMEM_LIMIT = 56 * 1024 * 1024

TM = 512
MLSTM_TS = 512
MLSTM_L = 256
ATTN_TQ = 256

_NT = (((1,), (1,)), ((), ()))


def _params(sem):
    return pltpu.CompilerParams(dimension_semantics=sem, vmem_limit_bytes=VMEM_LIMIT)


def _resident(shape):
    zeros = (0,) * len(shape)
    return pl.BlockSpec(shape, lambda *_: zeros, pipeline_mode=pl.Buffered(1))


def _rms(t, g, n):
    ms = jnp.sum(t * t, axis=-1, keepdims=True) * (1.0 / n)
    return t * lax.rsqrt(ms + EPS) * g


def _silu(t):
    return t * jax.nn.sigmoid(t)


def _dot(a, b):
    return jnp.dot(a, b, preferred_element_type=F32)


def _memkv_kernel(mem_ref, g_ref, w_ref, kg_ref, mk_ref, mv_ref):
    mn = _rms(mem_ref[0], g_ref[...], D_MODEL).astype(BF16)
    kv = _dot(mn, w_ref[0])
    for h in range(MEM_HEADS):
        sl = slice(h * MEM_HEAD_DIM, (h + 1) * MEM_HEAD_DIM)
        mk_ref[0, 0, :, sl] = _rms(kv[:, sl], kg_ref[0], MEM_HEAD_DIM).astype(BF16)
    mv_ref[0, 0] = kv[:, MEM_WIDTH:].astype(BF16)


def _memory_kv(mem, mem_norm, w_kv, k_gain):
    bsz, nm, _ = mem.shape
    out = jax.ShapeDtypeStruct((DEPTH, bsz, nm, MEM_WIDTH), BF16)
    return pl.pallas_call(
        _memkv_kernel,
        grid=(DEPTH, bsz),
        in_specs=[
            pl.BlockSpec((1, nm, D_MODEL), lambda l, b: (b, 0, 0)),
            pl.BlockSpec((1, D_MODEL), lambda l, b: (0, 0)),
            pl.BlockSpec((1, D_MODEL, 2 * MEM_WIDTH), lambda l, b: (l, 0, 0)),
            pl.BlockSpec((1, 1, MEM_HEAD_DIM), lambda l, b: (l, 0, 0)),
        ],
        out_specs=[
            pl.BlockSpec((1, 1, nm, MEM_WIDTH), lambda l, b: (l, b, 0, 0)),
            pl.BlockSpec((1, 1, nm, MEM_WIDTH), lambda l, b: (l, b, 0, 0)),
        ],
        out_shape=[out, out],
        compiler_params=_params(("arbitrary", "arbitrary")),
        name="memory_kv",
    )(mem, mem_norm.reshape(1, D_MODEL), w_kv.astype(BF16), k_gain.reshape(DEPTH, 1, MEM_HEAD_DIM))


def _inproj_a_kernel(x_ref, g_ref, wmain_ref, wgate_ref, wmem_ref, cw_ref, cb_ref,
                     mix_ref, gate_ref, mm_ref, tail_ref, *, tiles_per_seq):
    tm = x_ref.shape[0]
    cn = 2 * LANES
    row = lax.broadcasted_iota(jnp.int32, (SUBLANES, cn), 0)

    @pl.when(pl.program_id(0) % tiles_per_seq == 0)
    def _():
        tail_ref[...] = jnp.zeros_like(tail_ref)

    h = _rms(x_ref[...], g_ref[...], D_MODEL).astype(BF16)

    def conv_chunk(c):
        cs = slice(c * cn, (c + 1) * cn)
        u = _dot(h, wmain_ref[:, cs])
        u3 = jnp.concatenate([tail_ref[:, cs], u], axis=0).reshape(tm // SUBLANES + 1, SUBLANES, cn)
        cur = u3[1:]
        acc = cb_ref[:, cs] + cw_ref[CONV_W - 1:CONV_W, cs] * cur
        for k in range(1, CONV_W):
            r = pltpu.roll(u3, k, 1)
            shifted = jnp.where(row < k, r[:-1], r[1:])
            acc = acc + cw_ref[CONV_W - 1 - k:CONV_W - k, cs] * shifted
        tail_ref[:, cs] = u[tm - SUBLANES:tm, :]
        scale = M_HEAD_DIM ** -0.5 if c * cn < M_WIDTH else 1.0
        mix_ref[:, cs] = (_silu(acc) * scale).reshape(tm, cn).astype(BF16)

    def value_chunk(c):
        cs = slice(2 * M_WIDTH + c * cn, 2 * M_WIDTH + (c + 1) * cn)
        mix_ref[:, cs] = _dot(h, wmain_ref[:, cs]).astype(BF16)

    def gate_chunk(c):
        o = _dot(h, wmain_ref[:, 3 * M_WIDTH + c * cn:3 * M_WIDTH + (c + 1) * cn])
        z = _dot(h, wmain_ref[:, 4 * M_WIDTH + c * cn:4 * M_WIDTH + (c + 1) * cn])
        mix_ref[:, 3 * M_WIDTH + c * cn:3 * M_WIDTH + (c + 1) * cn] = (
            jax.nn.sigmoid(o) * _silu(z)).astype(BF16)

    def mem_chunk(c):
        cs = slice(c * cn, (c + 1) * cn)
        mm_ref[:, cs] = _dot(h, wmem_ref[:, cs]).astype(BF16)

    for c in range(M_WIDTH // cn):
        conv_chunk(2 * c)
        value_chunk(c)
        conv_chunk(2 * c + 1)
        gate_chunk(c)
        mem_chunk(c)
    gate_ref[...] = _dot(h, wgate_ref[...])


def _inproj_a(x2d, seq, norm_g, w_in, conv_w, conv_b):
    t = x2d.shape[0]
    wmain = w_in[:, :A_MAIN].astype(BF16)
    wgate = jnp.pad(w_in[:, A_MAIN:A_MAIN + 2 * M_HEADS], ((0, 0), (0, LANES - 2 * M_HEADS))).astype(BF16)
    wmem = w_in[:, A_MAIN + 2 * M_HEADS:].astype(BF16)
    return pl.pallas_call(
        functools.partial(_inproj_a_kernel, tiles_per_seq=seq // TM),
        grid=(t // TM,),
        in_specs=[
            pl.BlockSpec((TM, D_MODEL), lambda i: (i, 0)),
            _resident((1, D_MODEL)),
            _resident((D_MODEL, A_MAIN)),
            _resident((D_MODEL, LANES)),
            _resident((D_MODEL, 2 * MEM_WIDTH)),
            _resident((CONV_W, 2 * M_WIDTH)),
            _resident((1, 2 * M_WIDTH)),
        ],
        out_specs=[
            pl.BlockSpec((TM, A_MIX), lambda i: (i, 0)),
            pl.BlockSpec((TM, LANES), lambda i: (i, 0)),
            pl.BlockSpec((TM, 2 * MEM_WIDTH), lambda i: (i, 0)),
        ],
        out_shape=[
            jax.ShapeDtypeStruct((t, A_MIX), BF16),
            jax.ShapeDtypeStruct((t, LANES), F32),
            jax.ShapeDtypeStruct((t, 2 * MEM_WIDTH), BF16),
        ],
        scratch_shapes=[pltpu.VMEM((SUBLANES, 2 * M_WIDTH), F32)],
        compiler_params=_params(("arbitrary",)),
        name="inproj_a",
    )(x2d, norm_g.reshape(1, D_MODEL), wmain, wgate, wmem, conv_w, conv_b.reshape(1, 2 * M_WIDTH))


def _mlstm_kernel(mix_ref, gate_ref, gbias_ref, hg_ref, y_ref, ct_ref, n_ref, m_ref):
    ts, ln, dh = MLSTM_TS, MLSTM_L, M_HEAD_DIM
    heads = range(M_HEADS)

    @pl.when(pl.program_id(1) == 0)
    def _():
        ct_ref[...] = jnp.zeros_like(ct_ref)
        n_ref[...] = jnp.zeros_like(n_ref)
        m_ref[...] = jnp.zeros_like(m_ref)

    def cols(group, h):
        return slice(group * M_WIDTH + h * dh, group * M_WIDTH + (h + 1) * dh)

    def scores(c):
        rs = slice(c * ln, (c + 1) * ln)
        return [lax.dot_general(mix_ref[0, rs, cols(0, h)], mix_ref[0, rs, cols(1, h)], _NT,
                                preferred_element_type=F32) for h in heads]

    qk = scores(0)

    g = gate_ref[0] + gbias_ref[...]
    lsg = jnp.minimum(g, 0.0) - jnp.log1p(jnp.exp(-jnp.abs(g)))
    tri = (lax.broadcasted_iota(jnp.int32, (ln, ln), 0)
           >= lax.broadcasted_iota(jnp.int32, (ln, ln), 1))
    tri_b = jnp.where(tri, 1.0, 0.0).astype(BF16)
    p0 = lsg.astype(BF16)
    r0 = lsg - p0.astype(F32)
    p1 = r0.astype(BF16)
    p2 = (r0 - p1.astype(F32)).astype(BF16)
    pieces = jnp.concatenate([p0, p1, p2], axis=1)
    bcum = []
    for c in range(ts // ln):
        part = _dot(tri_b, pieces[c * ln:(c + 1) * ln])
        bcum.append(part[:, 0:LANES] + part[:, LANES:2 * LANES] + part[:, 2 * LANES:])
    bcum = jnp.concatenate(bcum, axis=0)
    g_t = g.T
    b_t = bcum.T

    m_st = [m_ref[h] for h in heads]
    n_st = [n_ref[h] for h in heads]

    for c in range(ts // ln):
        rs = slice(c * ln, (c + 1) * ln)
        s_l, gi_l, mt_l, kwt_l, ksum_l, gs_l, mnew_l = [], [], [], [], [], [], []
        for h in heads:
            b_col = bcum[rs, M_HEADS + h:M_HEADS + h + 1]
            b_row = b_t[M_HEADS + h:M_HEADS + h + 1, rs]
            i_col = g[rs, h:h + 1]
            i_row = g_t[h:h + 1, rs]
            m_prev = m_st[h]
            d = jnp.where(tri, b_col - b_row + i_row, -jnp.inf)
            inter = b_col + m_prev
            m_t = jnp.maximum(inter, jnp.max(d, axis=1, keepdims=True))
            s_l.append(qk[h] * jnp.exp(d - m_t))
            gi_l.append(jnp.exp(inter - m_t))
            mt_l.append(m_t)
            b_last = b_col[ln - 1:ln, :]
            a_col = b_last - b_col + i_col
            m_new = jnp.maximum(b_last + m_prev, jnp.max(a_col, axis=0, keepdims=True))
            kw = mix_ref[0, rs, cols(1, h)].astype(F32) * jnp.exp(a_col - m_new)
            kwt_l.append(kw.T.astype(BF16))
            ksum_l.append(jnp.sum(kw, axis=0, keepdims=True))
            gs_l.append(jnp.exp(b_last + m_prev - m_new))
            mnew_l.append(m_new)

        sv_l, qc_l, up_l = [], [], []
        for h in heads:
            vb = mix_ref[0, rs, cols(2, h)]
            sv_l.append(_dot(s_l[h].astype(BF16), vb))
            qc_l.append(_dot(mix_ref[0, rs, cols(0, h)], ct_ref[h].astype(BF16)))
            up_l.append(_dot(kwt_l[h], vb))
        if c + 1 < ts // ln:
            qk = scores(c + 1)

        for h in heads:
            qf = mix_ref[0, rs, cols(0, h)].astype(F32)
            num = sv_l[h] + gi_l[h] * qc_l[h]
            den = (jnp.sum(s_l[h], axis=1, keepdims=True)
                   + gi_l[h] * jnp.sum(qf * n_st[h], axis=1, keepdims=True))
            hout = num * (1.0 / jnp.maximum(jnp.abs(den), jnp.exp(-mt_l[h])))
            ct_ref[h] = gs_l[h] * ct_ref[h] + up_l[h]
            n_st[h] = gs_l[h] * n_st[h] + ksum_l[h]
            m_st[h] = mnew_l[h]
            ht = _rms(hout, hg_ref[:, cols(0, h)], dh)
            gate = mix_ref[0, rs, cols(3, h)].astype(F32)
            y_ref[0, rs, cols(0, h)] = (ht * gate).astype(BF16)

    for h in heads:
        n_ref[h] = n_st[h]
        m_ref[h] = m_st[h]


def _mlstm(mix, gates, ig_b, fg_b, h_g):
    bsz, s, _ = mix.shape
    ts = MLSTM_TS
    gbias = jnp.concatenate([ig_b, fg_b, jnp.zeros((LANES - 2 * M_HEADS,), F32)]).reshape(1, LANES)
    return pl.pallas_call(
        _mlstm_kernel,
        grid=(bsz, s // ts),
        in_specs=[
            pl.BlockSpec((1, ts, A_MIX), lambda b, t: (b, t, 0)),
            pl.BlockSpec((1, ts, LANES), lambda b, t: (b, t, 0)),
            _resident((1, LANES)),
            _resident((1, M_WIDTH)),
        ],
        out_specs=pl.BlockSpec((1, ts, M_WIDTH), lambda b, t: (b, t, 0)),
        out_shape=jax.ShapeDtypeStruct((bsz, s, M_WIDTH), BF16),
        scratch_shapes=[
            pltpu.VMEM((M_HEADS, M_HEAD_DIM, M_HEAD_DIM), F32),
            pltpu.VMEM((M_HEADS, 1, M_HEAD_DIM), F32),
            pltpu.VMEM((M_HEADS, 1, 1), F32),
        ],
        compiler_params=_params(("arbitrary", "arbitrary")),
        name="mlstm",
    )(mix, gates, gbias, h_g.reshape(1, M_WIDTH))


def _outproj_kernel(y_ref, mm_ref, mk_ref, mv_ref, qg_ref, w_ref, x_ref, o_ref):
    half = D_MODEL // 2
    heads = range(MEM_HEADS)
    hsl = lambda h: slice(h * MEM_HEAD_DIM, (h + 1) * MEM_HEAD_DIM)
    y = y_ref[0]
    acc_lo = _dot(y, w_ref[0:M_WIDTH, 0:half])
    qs = [_rms(mm_ref[0, :, hsl(h)].astype(F32), qg_ref[...], MEM_HEAD_DIM).astype(BF16) for h in heads]
    sc = [lax.dot_general(qs[h], mk_ref[0, 0, :, hsl(h)], _NT, preferred_element_type=F32) for h in heads]
    acc_hi = _dot(y, w_ref[0:M_WIDTH, half:])
    ps = [jnp.exp2(sc[h] - jnp.max(sc[h], axis=-1, keepdims=True)) for h in heads]
    ymem = []
    for h in heads:
        att = _dot(ps[h].astype(BF16), mv_ref[0, 0, :, hsl(h)]) * (1.0 / jnp.sum(ps[h], axis=-1, keepdims=True))
        z = mm_ref[0, :, MEM_WIDTH + h * MEM_HEAD_DIM:MEM_WIDTH + (h + 1) * MEM_HEAD_DIM].astype(F32)
        ymem.append((att * _silu(z)).astype(BF16))
    ymem = jnp.concatenate(ymem, axis=-1)
    o_ref[0, :, 0:half] = x_ref[0, :, 0:half] + acc_lo + _dot(ymem, w_ref[M_WIDTH:, 0:half])
    o_ref[0, :, half:] = x_ref[0, :, half:] + acc_hi + _dot(ymem, w_ref[M_WIDTH:, half:])


def _outproj(y, mm, mk_all, mv_all, layer, q_gain, w_out, x):
    bsz, s, _ = x.shape
    nm = mk_all.shape[2]
    kv_spec = pl.BlockSpec((1, 1, nm, MEM_WIDTH), lambda b, t: (layer, b, 0, 0))
    tile = lambda w: pl.BlockSpec((1, TM, w), lambda b, t: (b, t, 0))
    qg = (q_gain * (MEM_HEAD_DIM ** -0.5 * LOG2E)).reshape(1, MEM_HEAD_DIM)
    return pl.pallas_call(
        _outproj_kernel,
        grid=(bsz, s // TM),
        in_specs=[
            tile(M_WIDTH), tile(2 * MEM_WIDTH), kv_spec, kv_spec,
            _resident((1, MEM_HEAD_DIM)),
            _resident((M_WIDTH + MEM_WIDTH, D_MODEL)),
            tile(D_MODEL),
        ],
        out_specs=tile(D_MODEL),
        out_shape=jax.ShapeDtypeStruct((bsz, s, D_MODEL), F32),
        compiler_params=_params(("arbitrary", "arbitrary")),
        name="outproj",
    )(y, mm, mk_all, mv_all, qg, w_out.astype(BF16), x)


def _norm_rope(t, gain, cos_t, sin_t):
    half = QK_ROPE // 2
    t = _rms(t, gain, QK_HEAD)
    r = t[:, QK_NOPE:]
    rot = pltpu.roll(r, half, 1) + pltpu.roll(r, LANES - half, 1)
    return t[:, :QK_NOPE], r * cos_t + rot * sin_t


def _inproj_b_kernel(x_ref, cos_ref, sin_ref, g_ref, win_ref, qlg_ref, wq_ref, qg_ref,
                     q_ref, zz_ref, mm_ref):
    h = _rms(x_ref[0], g_ref[...], D_MODEL).astype(BF16)
    q_lat = _dot(h, win_ref[:, 0:Q_LORA])
    zz_ref[0, :, 0:B_WIDTH // 2] = _dot(h, win_ref[:, Q_LORA:Q_LORA + B_WIDTH // 2]).astype(BF16)
    ql = _rms(q_lat, qlg_ref[...], Q_LORA).astype(BF16)
    cos_t = cos_ref[0]
    sin_t = sin_ref[0]
    fill = 2 * LANES
    rest = Q_LORA + B_WIDTH // 2
    for hd in range(B_HEADS):
        t = _dot(ql, wq_ref[:, hd * QK_PAD:(hd + 1) * QK_PAD])
        lo = rest + hd * fill
        if lo >= win_ref.shape[1]:
            pass
        elif lo < Q_LORA + B_WIDTH:
            zz_ref[0, :, lo - Q_LORA:lo - Q_LORA + fill] = _dot(h, win_ref[:, lo:lo + fill]).astype(BF16)
        else:
            mo = lo - Q_LORA - B_WIDTH
            mm_ref[0, :, mo:mo + fill] = _dot(h, win_ref[:, lo:lo + fill]).astype(BF16)
        nope, rot = _norm_rope(t, qg_ref[...], cos_t, sin_t)
        q_ref[0, hd, :, 0:QK_NOPE] = nope.astype(BF16)
        q_ref[0, hd, :, QK_NOPE:] = rot.astype(BF16)


def _pad_heads(w, heads, width):
    lead = w.shape[:-1]
    w = w.reshape(lead + (heads, width))
    w = jnp.pad(w, [(0, 0)] * len(lead) + [(0, 0), (0, QK_PAD - width)])
    return w.reshape(lead + (heads * QK_PAD,))


def _rope_const():
    half = QK_ROPE // 2
    inv = ROPE_THETA ** (-jnp.arange(0, QK_ROPE, 2, dtype=F32) / QK_ROPE)
    zeros = jnp.zeros((LANES - QK_ROPE,), F32)
    ones = jnp.ones((half,), F32)
    rows = [jnp.concatenate([inv, inv, zeros]),
            jnp.concatenate([ones, ones, zeros]),
            jnp.concatenate([-ones, ones, zeros])]
    return jnp.concatenate([jnp.stack(rows), jnp.zeros((SUBLANES - 3, LANES), F32)], axis=0)


def _inproj_b(x, cos_t, sin_t, norm_g, w_in, q_lat_g, w_q_up, q_gain):
    bsz, s, _ = x.shape
    b_in = w_in.shape[1]
    tile = lambda w: pl.BlockSpec((1, TM, w), lambda b, t: (b, t, 0))
    qg = jnp.pad(q_gain * (QK_HEAD ** -0.5 * LOG2E), (0, QK_PAD - QK_HEAD)).reshape(1, QK_PAD)
    return pl.pallas_call(
        _inproj_b_kernel,
        grid=(bsz, s // TM),
        in_specs=[
            tile(D_MODEL), tile(LANES), tile(LANES),
            _resident((1, D_MODEL)),
            _resident((D_MODEL, b_in)),
            _resident((1, Q_LORA)),
            _resident((Q_LORA, B_HEADS * QK_PAD)),
            _resident((1, QK_PAD)),
        ],
        out_specs=[
            pl.BlockSpec((1, B_HEADS, TM, QK_PAD), lambda b, t: (b, 0, t, 0)),
            tile(B_WIDTH), tile(2 * MEM_WIDTH),
        ],
        out_shape=[
            jax.ShapeDtypeStruct((bsz, B_HEADS, s, QK_PAD), BF16),
            jax.ShapeDtypeStruct((bsz, s, B_WIDTH), BF16),
            jax.ShapeDtypeStruct((bsz, s, 2 * MEM_WIDTH), BF16),
        ],
        compiler_params=_params(("arbitrary", "arbitrary")),
        name="inproj_b",
    )(x, cos_t, sin_t, norm_g.reshape(1, D_MODEL), w_in.astype(BF16),
      q_lat_g.reshape(1, Q_LORA), _pad_heads(w_q_up, B_HEADS, QK_HEAD).astype(BF16), qg)


def _sharedkv_kernel(x_ref, pos_ref, tab_ref, g_ref, wa_ref, lg_ref, wb_ref, kg_ref,
                     k_ref, v_ref, cos_ref, sin_ref):
    h = _rms(x_ref[0], g_ref[...], D_MODEL).astype(BF16)
    a = _dot(h, wa_ref[...])
    c = _rms(a[:, 0:KV_LORA], lg_ref[...], KV_LORA).astype(BF16)
    k_pe = a[:, KV_LORA:]
    ang = pos_ref[0].astype(F32) * tab_ref[0:1, :]
    cos_t = jnp.cos(ang) * tab_ref[1:2, :]
    sin_t = jnp.sin(ang) * tab_ref[2:3, :]
    cos_ref[0] = cos_t
    sin_ref[0] = sin_t
    for hd in range(B_HEADS):
        kv = _dot(c, wb_ref[:, hd * (QK_NOPE + V_HEAD):(hd + 1) * (QK_NOPE + V_HEAD)])
        t = jnp.concatenate([kv[:, 0:QK_NOPE], k_pe], axis=-1)
        nope, rot = _norm_rope(t, kg_ref[...], cos_t, sin_t)
        k_ref[0, hd, :, 0:QK_NOPE] = nope.astype(BF16)
        k_ref[0, hd, :, QK_NOPE:] = rot.astype(BF16)
        v_ref[0, hd] = kv[:, QK_NOPE:].astype(BF16)


def _shared_kv(x, pos3, kv_norm, w_kv_a, kv_lat_norm, w_kv_b, k_gain):
    bsz, s, _ = x.shape
    a_w = KV_LORA + LANES
    tile = lambda w: pl.BlockSpec((1, TM, w), lambda b, t: (b, t, 0))
    table = jax.ShapeDtypeStruct((bsz, s, LANES), F32)
    return pl.pallas_call(
        _sharedkv_kernel,
        grid=(bsz, s // TM),
        in_specs=[
            tile(D_MODEL), tile(1),
            _resident((SUBLANES, LANES)),
            _resident((1, D_MODEL)),
            _resident((D_MODEL, a_w)),
            _resident((1, KV_LORA)),
            _resident((KV_LORA, B_HEADS * (QK_NOPE + V_HEAD))),
            _resident((1, QK_PAD)),
        ],
        out_specs=[
            pl.BlockSpec((1, B_HEADS, TM, QK_PAD), lambda b, t: (b, 0, t, 0)),
            pl.BlockSpec((1, B_HEADS, TM, V_HEAD), lambda b, t: (b, 0, t, 0)),
            tile(LANES), tile(LANES),
        ],
        out_shape=[
            jax.ShapeDtypeStruct((bsz, B_HEADS, s, QK_PAD), BF16),
            jax.ShapeDtypeStruct((bsz, B_HEADS, s, V_HEAD), BF16),
            table, table,
        ],
        compiler_params=_params(("arbitrary", "arbitrary")),
        name="shared_kv",
    )(x, pos3, _rope_const(), kv_norm.reshape(1, D_MODEL),
      jnp.pad(w_kv_a, ((0, 0), (0, a_w - w_kv_a.shape[1]))).astype(BF16),
      kv_lat_norm.reshape(1, KV_LORA), w_kv_b.astype(BF16),
      jnp.pad(k_gain, (0, QK_PAD - QK_HEAD)).reshape(1, QK_PAD))


def _attn_kernel(q_ref, k_ref, v_ref, z_ref, o_ref):
    tq = ATTN_TQ
    s = q_ref.shape[2]
    sh = CHUNK.bit_length() - 1
    rc = lax.broadcasted_iota(jnp.int32, (tq, tq), 0) >> sh
    cc = lax.broadcasted_iota(jnp.int32, (tq, tq), 1) >> sh
    dmask = cc <= rc
    nq = s // tq

    def scores(qi):
        rows = slice(qi * tq, (qi + 1) * tq)
        q = q_ref[0, 0, rows, :]
        sd = lax.dot_general(q, k_ref[0, 0, rows, :], _NT, preferred_element_type=F32)
        so = None
        if qi > 0:
            so = lax.dot_general(q, k_ref[0, 0, 0:qi * tq, :], _NT, preferred_element_type=F32)
        return sd, so

    nxt = scores(0)
    for qi in range(nq):
        rows = slice(qi * tq, (qi + 1) * tq)
        sd, so = nxt
        if qi + 1 < nq:
            nxt = scores(qi + 1)
        sd = jnp.where(dmask, sd, -jnp.inf)
        m = jnp.max(sd, axis=-1, keepdims=True)
        if qi > 0:
            past = slice(0, qi * tq)
            m = jnp.maximum(m, jnp.max(so, axis=-1, keepdims=True))
            po = jnp.exp2(so - m)
            l = jnp.sum(po, axis=-1, keepdims=True)
            acc = _dot(po.astype(BF16), v_ref[0, 0, past, :])
        pd = jnp.exp2(sd - m)
        if qi > 0:
            l = l + jnp.sum(pd, axis=-1, keepdims=True)
            acc = acc + _dot(pd.astype(BF16), v_ref[0, 0, rows, :])
        else:
            l = jnp.sum(pd, axis=-1, keepdims=True)
            acc = _dot(pd.astype(BF16), v_ref[0, 0, rows, :])
        o_ref[0, rows, :] = (acc * (1.0 / l) * _silu(z_ref[0, rows, :].astype(F32))).astype(BF16)


def _attention(q, k, v, zz):
    bsz, heads, s, _ = q.shape
    head_spec = lambda w: pl.BlockSpec((1, 1, s, w), lambda b, h: (b, h, 0, 0))
    col_spec = pl.BlockSpec((1, s, V_HEAD), lambda b, h: (b, 0, h))
    return pl.pallas_call(
        _attn_kernel,
        grid=(bsz, heads),
        in_specs=[head_spec(QK_PAD), head_spec(QK_PAD), head_spec(V_HEAD), col_spec],
        out_specs=col_spec,
        out_shape=jax.ShapeDtypeStruct((bsz, s, B_WIDTH), BF16),
        compiler_params=_params(("arbitrary", "arbitrary")),
        name="mla_attention",
    )(q, k, v, zz)


def kernel(x, mem, positions, a_norm, a_w_in, a_conv_w, a_conv_b, a_ig_bias, a_fg_bias, a_h_norm, a_w_out,
           b_norm, b_w_in, b_q_lat_norm, b_w_q_up, b_q_gain, b_w_out,
           kv_norm, w_kv_a, kv_lat_norm, w_kv_b, k_gain,
           mem_norm, mem_w_kv, mem_q_gain, mem_k_gain):
    bsz, s, _ = x.shape
    assert s % MLSTM_TS == 0 and s % TM == 0 and s % ATTN_TQ == 0 and MLSTM_TS % MLSTM_L == 0
    mk_all, mv_all = _memory_kv(mem, mem_norm, mem_w_kv, mem_k_gain)
    pos3 = positions.reshape(bsz, s, 1)
    k_sh = v_sh = cos_t = sin_t = None
    for layer in range(DEPTH):
        if layer < N_A:
            mix, gates, mm = _inproj_a(x.reshape(bsz * s, D_MODEL), s, a_norm[layer], a_w_in[layer],
                                       a_conv_w[layer], a_conv_b[layer])
            y = _mlstm(mix.reshape(bsz, s, A_MIX), gates.reshape(bsz, s, LANES),
                       a_ig_bias[layer], a_fg_bias[layer], a_h_norm[layer])
            x = _outproj(y, mm.reshape(bsz, s, 2 * MEM_WIDTH), mk_all, mv_all, layer,
                         mem_q_gain[layer], a_w_out[layer], x)
        else:
            j = layer - N_A
            if k_sh is None:
                k_sh, v_sh, cos_t, sin_t = _shared_kv(x, pos3, kv_norm, w_kv_a, kv_lat_norm, w_kv_b, k_gain)
            q, zz, mm = _inproj_b(x, cos_t, sin_t, b_norm[j], b_w_in[j], b_q_lat_norm[j], b_w_q_up[j],
                                  b_q_gain[j])
            y = _attention(q, k_sh, v_sh, zz)
            x = _outproj(y, mm, mk_all, mv_all, layer, mem_q_gain[layer], b_w_out[j], x)
    return x
```

```python
import functools

import jax
import jax.numpy as jnp
from jax import lax
from jax.experimental import pallas as pl
from jax.experimental.pallas import tpu as pltpu

F32 = jnp.float32
BF16 = jnp.bfloat16

D_MODEL = 1024
DEPTH = 4
N_A = DEPTH // 2
CHUNK = 64
EPS = 1e-6
M_HEADS = 4
M_HEAD_DIM = 256
M_WIDTH = 1024
CONV_W = 4
B_HEADS = 8
QK_NOPE = 128
QK_ROPE = 64
QK_HEAD = 192
QK_PAD = 256
V_HEAD = 128
B_WIDTH = 1024
Q_LORA = 384
KV_LORA = 256
ROPE_THETA = 10000.0
MEM_HEADS = 4
MEM_HEAD_DIM = 128
MEM_WIDTH = 512
A_MAIN = 5 * M_WIDTH
A_MIX = 4 * M_WIDTH
LANES = 128
SUBLANES = 8
LOG2E = 1.4426950408889634

VMEM_LIMIT = 56 * 1024 * 1024

TM = 512
TM_A = 512
MLSTM_TS = 512
MLSTM_L = 256
ATTN_TQ = 256

_NT = (((1,), (1,)), ((), ()))


def _params(sem):
    return pltpu.CompilerParams(dimension_semantics=sem, vmem_limit_bytes=VMEM_LIMIT)


def _resident(shape):
    zeros = (0,) * len(shape)
    return pl.BlockSpec(shape, lambda *_: zeros, pipeline_mode=pl.Buffered(1))


def _rms(t, g, n):
    ms = jnp.sum(t * t, axis=-1, keepdims=True) * (1.0 / n)
    return t * lax.rsqrt(ms + EPS) * g


def _silu(t):
    return t * jax.nn.sigmoid(t)


def _dot(a, b):
    return jnp.dot(a, b, preferred_element_type=F32)


def _memkv_kernel(mem_ref, g_ref, w_ref, kg_ref, mk_ref, mv_ref):
    mn = _rms(mem_ref[0], g_ref[...], D_MODEL).astype(BF16)
    kv = _dot(mn, w_ref[0])
    for h in range(MEM_HEADS):
        sl = slice(h * MEM_HEAD_DIM, (h + 1) * MEM_HEAD_DIM)
        mk_ref[0, 0, :, sl] = _rms(kv[:, sl], kg_ref[0], MEM_HEAD_DIM).astype(BF16)
    mv_ref[0, 0] = kv[:, MEM_WIDTH:].astype(BF16)


def _memory_kv(mem, mem_norm, w_kv, k_gain):
    bsz, nm, _ = mem.shape
    out = jax.ShapeDtypeStruct((DEPTH, bsz, nm, MEM_WIDTH), BF16)
    return pl.pallas_call(
        _memkv_kernel,
        grid=(DEPTH, bsz),
        in_specs=[
            pl.BlockSpec((1, nm, D_MODEL), lambda l, b: (b, 0, 0)),
            pl.BlockSpec((1, D_MODEL), lambda l, b: (0, 0)),
            pl.BlockSpec((1, D_MODEL, 2 * MEM_WIDTH), lambda l, b: (l, 0, 0)),
            pl.BlockSpec((1, 1, MEM_HEAD_DIM), lambda l, b: (l, 0, 0)),
        ],
        out_specs=[
            pl.BlockSpec((1, 1, nm, MEM_WIDTH), lambda l, b: (l, b, 0, 0)),
            pl.BlockSpec((1, 1, nm, MEM_WIDTH), lambda l, b: (l, b, 0, 0)),
        ],
        out_shape=[out, out],
        compiler_params=_params(("arbitrary", "arbitrary")),
        name="memory_kv",
    )(mem, mem_norm.reshape(1, D_MODEL), w_kv.astype(BF16), k_gain.reshape(DEPTH, 1, MEM_HEAD_DIM))


def _inproj_a_kernel(x_ref, g_ref, wmain_ref, wgate_ref, wmem_ref, cw_ref, cb_ref,
                     mix_ref, gate_ref, mm_ref, tail_ref, *, tiles_per_seq):
    tm = x_ref.shape[0]
    cn = 2 * LANES
    row = lax.broadcasted_iota(jnp.int32, (SUBLANES, cn), 0)

    @pl.when(pl.program_id(0) % tiles_per_seq == 0)
    def _():
        tail_ref[...] = jnp.zeros_like(tail_ref)

    h = _rms(x_ref[...], g_ref[...], D_MODEL).astype(BF16)

    def conv_chunk(c):
        cs = slice(c * cn, (c + 1) * cn)
        u = _dot(h, wmain_ref[:, cs])
        u3 = jnp.concatenate([tail_ref[:, cs], u], axis=0).reshape(tm // SUBLANES + 1, SUBLANES, cn)
        acc = cb_ref[:, cs] + cw_ref[CONV_W - 1:CONV_W, cs] * u3[1:]
        for k in range(1, CONV_W):
            r = pltpu.roll(u3, k, 1)
            shifted = jnp.where(row < k, r[:-1], r[1:])
            acc = acc + cw_ref[CONV_W - 1 - k:CONV_W - k, cs] * shifted
        tail_ref[:, cs] = u[tm - SUBLANES:tm, :]
        scale = M_HEAD_DIM ** -0.5 if c * cn < M_WIDTH else 1.0
        mix_ref[:, cs] = (_silu(acc) * scale).reshape(tm, cn).astype(BF16)

    def value_chunk(c):
        cs = slice(2 * M_WIDTH + c * cn, 2 * M_WIDTH + (c + 1) * cn)
        mix_ref[:, cs] = _dot(h, wmain_ref[:, cs]).astype(BF16)

    def gate_chunk(c):
        o = _dot(h, wmain_ref[:, 3 * M_WIDTH + c * cn:3 * M_WIDTH + (c + 1) * cn])
        z = _dot(h, wmain_ref[:, 4 * M_WIDTH + c * cn:4 * M_WIDTH + (c + 1) * cn])
        mix_ref[:, 3 * M_WIDTH + c * cn:3 * M_WIDTH + (c + 1) * cn] = (
            jax.nn.sigmoid(o) * _silu(z)).astype(BF16)

    def mem_chunk(c):
        cs = slice(c * cn, (c + 1) * cn)
        mm_ref[:, cs] = _dot(h, wmem_ref[:, cs]).astype(BF16)

    for c in range(M_WIDTH // cn):
        conv_chunk(2 * c)
        value_chunk(c)
        conv_chunk(2 * c + 1)
        gate_chunk(c)
        mem_chunk(c)
    gate_ref[...] = _dot(h, wgate_ref[...])


def _inproj_a(x2d, seq, norm_g, w_in, conv_w, conv_b):
    t = x2d.shape[0]
    wmain = w_in[:, :A_MAIN].astype(BF16)
    wgate = jnp.pad(w_in[:, A_MAIN:A_MAIN + 2 * M_HEADS], ((0, 0), (0, LANES - 2 * M_HEADS))).astype(BF16)
    wmem = w_in[:, A_MAIN + 2 * M_HEADS:].astype(BF16)
    tm = TM_A
    return pl.pallas_call(
        functools.partial(_inproj_a_kernel, tiles_per_seq=seq // tm),
        grid=(t // tm,),
        in_specs=[
            pl.BlockSpec((tm, D_MODEL), lambda i: (i, 0)),
            _resident((1, D_MODEL)),
            _resident((D_MODEL, A_MAIN)),
            _resident((D_MODEL, LANES)),
            _resident((D_MODEL, 2 * MEM_WIDTH)),
            _resident((CONV_W, 2 * M_WIDTH)),
            _resident((1, 2 * M_WIDTH)),
        ],
        out_specs=[
            pl.BlockSpec((tm, A_MIX), lambda i: (i, 0)),
            pl.BlockSpec((tm, LANES), lambda i: (i, 0)),
            pl.BlockSpec((tm, 2 * MEM_WIDTH), lambda i: (i, 0)),
        ],
        out_shape=[
            jax.ShapeDtypeStruct((t, A_MIX), BF16),
            jax.ShapeDtypeStruct((t, LANES), F32),
            jax.ShapeDtypeStruct((t, 2 * MEM_WIDTH), BF16),
        ],
        scratch_shapes=[pltpu.VMEM((SUBLANES, 2 * M_WIDTH), F32)],
        compiler_params=_params(("arbitrary",)),
        name="inproj_a",
    )(x2d, norm_g.reshape(1, D_MODEL), wmain, wgate, wmem, conv_w, conv_b.reshape(1, 2 * M_WIDTH))


def _mlstm_kernel(mix_ref, gate_ref, gbias_ref, hg_ref, y_ref, ct_ref, n_ref, m_ref):
    ts, ln, dh = MLSTM_TS, MLSTM_L, M_HEAD_DIM
    heads = range(M_HEADS)

    @pl.when(pl.program_id(1) == 0)
    def _():
        ct_ref[...] = jnp.zeros_like(ct_ref)
        n_ref[...] = jnp.zeros_like(n_ref)
        m_ref[...] = jnp.zeros_like(m_ref)

    def cols(group, h):
        return slice(group * M_WIDTH + h * dh, group * M_WIDTH + (h + 1) * dh)

    def scores(c):
        rs = slice(c * ln, (c + 1) * ln)
        return [lax.dot_general(mix_ref[0, rs, cols(0, h)], mix_ref[0, rs, cols(1, h)], _NT,
                                preferred_element_type=F32) for h in heads]

    qk = scores(0)

    g = gate_ref[0] + gbias_ref[...]
    lsg = jnp.minimum(g, 0.0) - jnp.log1p(jnp.exp(-jnp.abs(g)))
    tri = (lax.broadcasted_iota(jnp.int32, (ln, ln), 0)
           >= lax.broadcasted_iota(jnp.int32, (ln, ln), 1))
    tri_b = jnp.where(tri, 1.0, 0.0).astype(BF16)
    p0 = lsg.astype(BF16)
    r0 = lsg - p0.astype(F32)
    p1 = r0.astype(BF16)
    p2 = (r0 - p1.astype(F32)).astype(BF16)
    pieces = jnp.concatenate([p0, p1, p2], axis=1)
    bcum = []
    for c in range(ts // ln):
        part = _dot(tri_b, pieces[c * ln:(c + 1) * ln])
        bcum.append(part[:, 0:LANES] + part[:, LANES:2 * LANES] + part[:, 2 * LANES:])
    bcum = jnp.concatenate(bcum, axis=0)
    g_t = g.T
    b_t = bcum.T

    m_st = [m_ref[h] for h in heads]
    n_st = [n_ref[h] for h in heads]

    for c in range(ts // ln):
        rs = slice(c * ln, (c + 1) * ln)
        qk_cur = qk
        if c + 1 < ts // ln:
            qk = scores(c + 1)
        qc_l = [_dot(mix_ref[0, rs, cols(0, h)], ct_ref[h].astype(BF16)) for h in heads]
        s_l, gi_l, mt_l, kwt_l, ksum_l, gs_l, mnew_l = [], [], [], [], [], [], []
        for h in heads:
            b_col = bcum[rs, M_HEADS + h:M_HEADS + h + 1]
            b_row = b_t[M_HEADS + h:M_HEADS + h + 1, rs]
            i_col = g[rs, h:h + 1]
            i_row = g_t[h:h + 1, rs]
            m_prev = m_st[h]
            d = jnp.where(tri, b_col - b_row + i_row, -jnp.inf)
            inter = b_col + m_prev
            m_t = jnp.maximum(inter, jnp.max(d, axis=1, keepdims=True))
            s_l.append(qk_cur[h] * jnp.exp(d - m_t))
            gi_l.append(jnp.exp(inter - m_t))
            mt_l.append(m_t)
            b_last = b_col[ln - 1:ln, :]
            a_col = b_last - b_col + i_col
            m_new = jnp.maximum(b_last + m_prev, jnp.max(a_col, axis=0, keepdims=True))
            kw = mix_ref[0, rs, cols(1, h)].astype(F32) * jnp.exp(a_col - m_new)
            kwt_l.append(kw.T.astype(BF16))
            ksum_l.append(jnp.sum(kw, axis=0, keepdims=True))
            gs_l.append(jnp.exp(b_last + m_prev - m_new))
            mnew_l.append(m_new)

        sv_l, up_l = [], []
        for h in heads:
            vb = mix_ref[0, rs, cols(2, h)]
            sv_l.append(_dot(s_l[h].astype(BF16), vb))
            up_l.append(_dot(kwt_l[h], vb))

        for h in heads:
            qf = mix_ref[0, rs, cols(0, h)].astype(F32)
            num = sv_l[h] + gi_l[h] * qc_l[h]
            den = (jnp.sum(s_l[h], axis=1, keepdims=True)
                   + gi_l[h] * jnp.sum(qf * n_st[h], axis=1, keepdims=True))
            hout = num * (1.0 / jnp.maximum(jnp.abs(den), jnp.exp(-mt_l[h])))
            ct_ref[h] = gs_l[h] * ct_ref[h] + up_l[h]
            n_st[h] = gs_l[h] * n_st[h] + ksum_l[h]
            m_st[h] = mnew_l[h]
            ht = _rms(hout, hg_ref[:, cols(0, h)], dh)
            gate = mix_ref[0, rs, cols(3, h)].astype(F32)
            y_ref[0, rs, cols(0, h)] = (ht * gate).astype(BF16)

    for h in heads:
        n_ref[h] = n_st[h]
        m_ref[h] = m_st[h]


def _mlstm(mix, gates, ig_b, fg_b, h_g):
    bsz, s, _ = mix.shape
    ts = MLSTM_TS
    gbias = jnp.concatenate([ig_b, fg_b, jnp.zeros((LANES - 2 * M_HEADS,), F32)]).reshape(1, LANES)
    return pl.pallas_call(
        _mlstm_kernel,
        grid=(bsz, s // ts),
        in_specs=[
            pl.BlockSpec((1, ts, A_MIX), lambda b, t: (b, t, 0)),
            pl.BlockSpec((1, ts, LANES), lambda b, t: (b, t, 0)),
            _resident((1, LANES)),
            _resident((1, M_WIDTH)),
        ],
        out_specs=pl.BlockSpec((1, ts, M_WIDTH), lambda b, t: (b, t, 0)),
        out_shape=jax.ShapeDtypeStruct((bsz, s, M_WIDTH), BF16),
        scratch_shapes=[
            pltpu.VMEM((M_HEADS, M_HEAD_DIM, M_HEAD_DIM), F32),
            pltpu.VMEM((M_HEADS, 1, M_HEAD_DIM), F32),
            pltpu.VMEM((M_HEADS, 1, 1), F32),
        ],
        compiler_params=_params(("arbitrary", "arbitrary")),
        name="mlstm",
    )(mix, gates, gbias, h_g.reshape(1, M_WIDTH))


def _outproj_kernel(y_ref, mm_ref, mk_ref, mv_ref, qg_ref, w_ref, x_ref, o_ref):
    half = D_MODEL // 2
    heads = range(MEM_HEADS)
    hsl = lambda h: slice(h * MEM_HEAD_DIM, (h + 1) * MEM_HEAD_DIM)
    y = y_ref[0]
    acc_lo = _dot(y, w_ref[0:M_WIDTH, 0:half])
    qs = [_rms(mm_ref[0, :, hsl(h)].astype(F32), qg_ref[...], MEM_HEAD_DIM).astype(BF16) for h in heads]
    sc = [lax.dot_general(qs[h], mk_ref[0, 0, :, hsl(h)], _NT, preferred_element_type=F32) for h in heads]
    acc_hi = _dot(y, w_ref[0:M_WIDTH, half:])
    ps = [jnp.exp2(sc[h] - jnp.max(sc[h], axis=-1, keepdims=True)) for h in heads]
    ymem = []
    for h in heads:
        att = _dot(ps[h].astype(BF16), mv_ref[0, 0, :, hsl(h)]) * (1.0 / jnp.sum(ps[h], axis=-1, keepdims=True))
        z = mm_ref[0, :, MEM_WIDTH + h * MEM_HEAD_DIM:MEM_WIDTH + (h + 1) * MEM_HEAD_DIM].astype(F32)
        ymem.append((att * _silu(z)).astype(BF16))
    ymem = jnp.concatenate(ymem, axis=-1)
    o_ref[0, :, 0:half] = x_ref[0, :, 0:half] + acc_lo + _dot(ymem, w_ref[M_WIDTH:, 0:half])
    o_ref[0, :, half:] = x_ref[0, :, half:] + acc_hi + _dot(ymem, w_ref[M_WIDTH:, half:])


def _outproj(y, mm, mk_all, mv_all, layer, q_gain, w_out, x):
    bsz, s, _ = x.shape
    nm = mk_all.shape[2]
    kv_spec = pl.BlockSpec((1, 1, nm, MEM_WIDTH), lambda b, t: (layer, b, 0, 0))
    tile = lambda w: pl.BlockSpec((1, TM, w), lambda b, t: (b, t, 0))
    qg = (q_gain * (MEM_HEAD_DIM ** -0.5 * LOG2E)).reshape(1, MEM_HEAD_DIM)
    return pl.pallas_call(
        _outproj_kernel,
        grid=(bsz, s // TM),
        in_specs=[
            tile(M_WIDTH), tile(2 * MEM_WIDTH), kv_spec, kv_spec,
            _resident((1, MEM_HEAD_DIM)),
            _resident((M_WIDTH + MEM_WIDTH, D_MODEL)),
            tile(D_MODEL),
        ],
        out_specs=tile(D_MODEL),
        out_shape=jax.ShapeDtypeStruct((bsz, s, D_MODEL), F32),
        compiler_params=_params(("arbitrary", "arbitrary")),
        name="outproj",
    )(y, mm, mk_all, mv_all, qg, w_out.astype(BF16), x)


def _rotate_half(r, cos_t, sin_t):
    half = QK_ROPE // 2
    return r * cos_t + (pltpu.roll(r, half, 1) + pltpu.roll(r, LANES - half, 1)) * sin_t


def _norm_rope_q(t, gain, cos_t, sin_t):
    nope = t[:, :QK_NOPE]
    r = t[:, QK_NOPE:]
    ss = jnp.sum(nope * nope, axis=-1, keepdims=True) + 0.5 * jnp.sum(r * r, axis=-1, keepdims=True)
    rs = lax.rsqrt(ss * (1.0 / QK_HEAD) + EPS)
    r = r * rs * gain[:, QK_NOPE:]
    return nope * rs * gain[:, :QK_NOPE], r * cos_t + pltpu.roll(r, QK_ROPE // 2, 1) * sin_t


def _inproj_b_kernel(x_ref, cos_ref, sin_ref, g_ref, win_ref, qlg_ref, wq_ref, qg_ref,
                     q_ref, zz_ref, mm_ref):
    h = _rms(x_ref[0], g_ref[...], D_MODEL).astype(BF16)
    q_lat = _dot(h, win_ref[:, 0:Q_LORA])
    zz_ref[0, :, 0:B_WIDTH // 2] = _dot(h, win_ref[:, Q_LORA:Q_LORA + B_WIDTH // 2]).astype(BF16)
    ql = _rms(q_lat, qlg_ref[...], Q_LORA).astype(BF16)
    cos_t = cos_ref[0]
    sin_t = sin_ref[0]
    heads = [_dot(ql, wq_ref[:, hd * QK_PAD:(hd + 1) * QK_PAD]) for hd in range(B_HEADS)]
    fill = 2 * LANES
    for lo in range(Q_LORA + B_WIDTH // 2, Q_LORA + B_WIDTH, fill):
        zz_ref[0, :, lo - Q_LORA:lo - Q_LORA + fill] = _dot(h, win_ref[:, lo:lo + fill]).astype(BF16)
    for lo in range(Q_LORA + B_WIDTH, win_ref.shape[1], fill):
        mo = lo - Q_LORA - B_WIDTH
        mm_ref[0, :, mo:mo + fill] = _dot(h, win_ref[:, lo:lo + fill]).astype(BF16)
    for hd in range(B_HEADS):
        nope, rot = _norm_rope_q(heads[hd], qg_ref[...], cos_t, sin_t)
        q_ref[0, hd, :, 0:QK_NOPE] = nope.astype(BF16)
        q_ref[0, hd, :, QK_NOPE:] = rot.astype(BF16)


def _dup_rope(w, heads):
    lead = w.shape[:-1]
    w = w.reshape(lead + (heads, QK_HEAD))
    w = jnp.concatenate([w, w[..., QK_NOPE:]], axis=-1)
    return w.reshape(lead + (heads * QK_PAD,))


def _rope_const():
    half = QK_ROPE // 2
    inv = ROPE_THETA ** (-jnp.arange(0, QK_ROPE, 2, dtype=F32) / QK_ROPE)
    zeros = jnp.zeros((LANES - QK_ROPE,), F32)
    ones = jnp.ones((half,), F32)
    rows = [jnp.concatenate([inv, inv, zeros]),
            jnp.concatenate([ones, ones, zeros]),
            jnp.concatenate([-ones, ones, zeros]),
            jnp.concatenate([0.0 * ones, (jnp.pi / 2) * ones, zeros])]
    return jnp.concatenate([jnp.stack(rows), jnp.zeros((SUBLANES - len(rows), LANES), F32)], axis=0)


def _inproj_b(x, cos_t, sin_t, norm_g, w_in, q_lat_g, w_q_up, q_gain):
    bsz, s, _ = x.shape
    b_in = w_in.shape[1]
    tile = lambda w: pl.BlockSpec((1, TM, w), lambda b, t: (b, t, 0))
    qg = _dup_rope(q_gain * (QK_HEAD ** -0.5 * LOG2E), 1).reshape(1, QK_PAD)
    return pl.pallas_call(
        _inproj_b_kernel,
        grid=(bsz, s // TM),
        in_specs=[
            tile(D_MODEL), tile(LANES), tile(LANES),
            _resident((1, D_MODEL)),
            _resident((D_MODEL, b_in)),
            _resident((1, Q_LORA)),
            _resident((Q_LORA, B_HEADS * QK_PAD)),
            _resident((1, QK_PAD)),
        ],
        out_specs=[
            pl.BlockSpec((1, B_HEADS, TM, QK_PAD), lambda b, t: (b, 0, t, 0)),
            tile(B_WIDTH), tile(2 * MEM_WIDTH),
        ],
        out_shape=[
            jax.ShapeDtypeStruct((bsz, B_HEADS, s, QK_PAD), BF16),
            jax.ShapeDtypeStruct((bsz, s, B_WIDTH), BF16),
            jax.ShapeDtypeStruct((bsz, s, 2 * MEM_WIDTH), BF16),
        ],
        compiler_params=_params(("arbitrary", "arbitrary")),
        name="inproj_b",
    )(x, cos_t, sin_t, norm_g.reshape(1, D_MODEL), w_in.astype(BF16),
      q_lat_g.reshape(1, Q_LORA), _dup_rope(w_q_up, B_HEADS).astype(BF16), qg)


def _sharedkv_kernel(x_ref, pos_ref, tab_ref, g_ref, wa_ref, lg_ref, wb_ref, kg_ref,
                     k_ref, vt_ref, cos_ref, sin_ref):
    half = QK_ROPE // 2
    h = _rms(x_ref[0], g_ref[...], D_MODEL).astype(BF16)
    a = _dot(h, wa_ref[...])
    c = _rms(a[:, 0:KV_LORA], lg_ref[...], KV_LORA).astype(BF16)
    k_pe = a[:, KV_LORA:]
    ang = pos_ref[0].astype(F32) * tab_ref[0:1, :] - tab_ref[3:4, :]
    cs = jnp.cos(ang)
    low = lax.broadcasted_iota(jnp.int32, cs.shape, 1) < half
    cos_t = jnp.where(low, cs, pltpu.roll(cs, half, 1)) * tab_ref[1:2, :]
    sin_t = jnp.where(low, pltpu.roll(cs, LANES - half, 1), cs) * tab_ref[2:3, :]
    cos_ref[0] = cos_t
    sin_ref[0] = sin_t
    pe_rot = _rotate_half(k_pe * kg_ref[:, QK_NOPE:], cos_t, sin_t)
    pe_ss = jnp.sum(k_pe * k_pe, axis=-1, keepdims=True)
    for hd in range(B_HEADS):
        kv = _dot(c, wb_ref[:, hd * (QK_NOPE + V_HEAD):(hd + 1) * (QK_NOPE + V_HEAD)])
        nope = kv[:, 0:QK_NOPE]
        rs = lax.rsqrt((jnp.sum(nope * nope, axis=-1, keepdims=True) + pe_ss) * (1.0 / QK_HEAD) + EPS)
        k_ref[0, hd, :, 0:QK_NOPE] = (nope * rs * kg_ref[:, 0:QK_NOPE]).astype(BF16)
        k_ref[0, hd, :, QK_NOPE:] = (pe_rot * rs).astype(BF16)
        vt_ref[0, hd] = kv[:, QK_NOPE:].T.astype(BF16)


def _shared_kv(x, pos3, kv_norm, w_kv_a, kv_lat_norm, w_kv_b, k_gain):
    bsz, s, _ = x.shape
    a_w = KV_LORA + LANES
    tile = lambda w: pl.BlockSpec((1, TM, w), lambda b, t: (b, t, 0))
    table = jax.ShapeDtypeStruct((bsz, s, LANES), F32)
    return pl.pallas_call(
        _sharedkv_kernel,
        grid=(bsz, s // TM),
        in_specs=[
            tile(D_MODEL), tile(1),
            _resident((SUBLANES, LANES)),
            _resident((1, D_MODEL)),
            _resident((D_MODEL, a_w)),
            _resident((1, KV_LORA)),
            _resident((KV_LORA, B_HEADS * (QK_NOPE + V_HEAD))),
            _resident((1, QK_PAD)),
        ],
        out_specs=[
            pl.BlockSpec((1, B_HEADS, TM, QK_PAD), lambda b, t: (b, 0, t, 0)),
            pl.BlockSpec((1, B_HEADS, V_HEAD, TM), lambda b, t: (b, 0, 0, t)),
            tile(LANES), tile(LANES),
        ],
        out_shape=[
            jax.ShapeDtypeStruct((bsz, B_HEADS, s, QK_PAD), BF16),
            jax.ShapeDtypeStruct((bsz, B_HEADS, V_HEAD, s), BF16),
            table, table,
        ],
        compiler_params=_params(("arbitrary", "arbitrary")),
        name="shared_kv",
    )(x, pos3, _rope_const(), kv_norm.reshape(1, D_MODEL),
      jnp.pad(w_kv_a, ((0, 0), (0, a_w - w_kv_a.shape[1]))).astype(BF16),
      kv_lat_norm.reshape(1, KV_LORA), w_kv_b.astype(BF16),
      jnp.pad(k_gain, (0, QK_PAD - QK_HEAD)).reshape(1, QK_PAD))


def _attn_kernel(q_ref, k_ref, vt_ref, z_ref, o_ref):
    tq = ATTN_TQ
    s = q_ref.shape[2]
    sh = CHUNK.bit_length() - 1
    kc = lax.broadcasted_iota(jnp.int32, (tq, tq), 0) >> sh
    qc = lax.broadcasted_iota(jnp.int32, (tq, tq), 1) >> sh
    dmask = kc <= qc
    nq = s // tq

    def scores(qi):
        rows = slice(qi * tq, (qi + 1) * tq)
        q = q_ref[0, 0, rows, :]
        sd = lax.dot_general(k_ref[0, 0, rows, :], q, _NT, preferred_element_type=F32)
        so = None
        if qi > 0:
            so = lax.dot_general(k_ref[0, 0, 0:qi * tq, :], q, _NT, preferred_element_type=F32)
        return sd, so

    def finish(qi, pd, po, l):
        rows = slice(qi * tq, (qi + 1) * tq)
        acc = _dot(vt_ref[0, 0, :, rows], pd)
        if po is not None:
            acc = acc + _dot(vt_ref[0, 0, :, 0:qi * tq], po)
        att = (acc * (1.0 / l)).T
        o_ref[0, rows, :] = (att * _silu(z_ref[0, rows, :].astype(F32))).astype(BF16)

    nxt = scores(0)
    pending = None
    for qi in range(nq):
        sd, so = nxt
        if qi + 1 < nq:
            nxt = scores(qi + 1)
        if pending is not None:
            finish(*pending)
        sd = jnp.where(dmask, sd, -jnp.inf)
        m = jnp.max(sd, axis=0, keepdims=True)
        po = None
        if qi > 0:
            m = jnp.maximum(m, jnp.max(so, axis=0, keepdims=True))
        pd = jnp.exp2(sd - m)
        l = jnp.sum(pd, axis=0, keepdims=True)
        if qi > 0:
            po = jnp.exp2(so - m)
            l = l + jnp.sum(po, axis=0, keepdims=True)
            po = po.astype(BF16)
        pending = (qi, pd.astype(BF16), po, l)
    finish(*pending)


def _attention(q, k, vt, zz):
    bsz, heads, s, _ = q.shape
    head_spec = pl.BlockSpec((1, 1, s, QK_PAD), lambda b, h: (b, h, 0, 0))
    vt_spec = pl.BlockSpec((1, 1, V_HEAD, s), lambda b, h: (b, h, 0, 0))
    col_spec = pl.BlockSpec((1, s, V_HEAD), lambda b, h: (b, 0, h))
    return pl.pallas_call(
        _attn_kernel,
        grid=(bsz, heads),
        in_specs=[head_spec, head_spec, vt_spec, col_spec],
        out_specs=col_spec,
        out_shape=jax.ShapeDtypeStruct((bsz, s, B_WIDTH), BF16),
        compiler_params=_params(("arbitrary", "arbitrary")),
        name="mla_attention",
    )(q, k, vt, zz)


def kernel(x, mem, positions, a_norm, a_w_in, a_conv_w, a_conv_b, a_ig_bias, a_fg_bias, a_h_norm, a_w_out,
           b_norm, b_w_in, b_q_lat_norm, b_w_q_up, b_q_gain, b_w_out,
           kv_norm, w_kv_a, kv_lat_norm, w_kv_b, k_gain,
           mem_norm, mem_w_kv, mem_q_gain, mem_k_gain):
    bsz, s, _ = x.shape
    assert s % MLSTM_TS == 0 and s % TM == 0 and s % ATTN_TQ == 0 and MLSTM_TS % MLSTM_L == 0
    mk_all, mv_all = _memory_kv(mem, mem_norm, mem_w_kv, mem_k_gain)
    pos3 = positions.reshape(bsz, s, 1)
    k_sh = v_sh = cos_t = sin_t = None
    for layer in range(DEPTH):
        if layer < N_A:
            mix, gates, mm = _inproj_a(x.reshape(bsz * s, D_MODEL), s, a_norm[layer], a_w_in[layer],
                                       a_conv_w[layer], a_conv_b[layer])
            y = _mlstm(mix.reshape(bsz, s, A_MIX), gates.reshape(bsz, s, LANES),
                       a_ig_bias[layer], a_fg_bias[layer], a_h_norm[layer])
            x = _outproj(y, mm.reshape(bsz, s, 2 * MEM_WIDTH), mk_all, mv_all, layer,
                         mem_q_gain[layer], a_w_out[layer], x)
        else:
            j = layer - N_A
            if k_sh is None:
                k_sh, v_sh, cos_t, sin_t = _shared_kv(x, pos3, kv_norm, w_kv_a, kv_lat_norm, w_kv_b, k_gain)
            q, zz, mm = _inproj_b(x, cos_t, sin_t, b_norm[j], b_w_in[j], b_q_lat_norm[j], b_w_q_up[j],
                                  b_q_gain[j])
            y = _attention(q, k_sh, v_sh, zz)
            x = _outproj(y, mm, mk_all, mv_all, layer, mem_q_gain[layer], b_w_out[j], x)
    return x
```

```python
import functools

import jax
import jax.numpy as jnp
from jax import lax
from jax.experimental import pallas as pl
from jax.experimental.pallas import tpu as pltpu

F32 = jnp.float32
BF16 = jnp.bfloat16

D_MODEL = 1024
DEPTH = 4
N_A = DEPTH // 2
CHUNK = 64
EPS = 1e-6
M_HEADS = 4
M_HEAD_DIM = 256
M_WIDTH = 1024
CONV_W = 4
B_HEADS = 8
QK_NOPE = 128
QK_ROPE = 64
QK_HEAD = 192
QK_PAD = 256
V_HEAD = 128
B_WIDTH = 1024
Q_LORA = 384
KV_LORA = 256
ROPE_THETA = 10000.0
MEM_HEADS = 4
MEM_HEAD_DIM = 128
MEM_WIDTH = 512
A_MAIN = 5 * M_WIDTH
A_MIX = 4 * M_WIDTH
LANES = 128
SUBLANES = 8
LOG2E = 1.4426950408889634

VMEM_LIMIT = 56 * 1024 * 1024

TM = 1024
TM_A = 1024
TM_B = 512
MLSTM_TS = 1024
MLSTM_L = 256
ATTN_TQ = 256

_NT = (((1,), (1,)), ((), ()))


def _params(sem):
    return pltpu.CompilerParams(dimension_semantics=sem, vmem_limit_bytes=VMEM_LIMIT)


def _resident(shape):
    zeros = (0,) * len(shape)
    return pl.BlockSpec(shape, lambda *_: zeros, pipeline_mode=pl.Buffered(1))


def _rms(t, g, n):
    ms = jnp.sum(t * t, axis=-1, keepdims=True) * (1.0 / n)
    return t * lax.rsqrt(ms + EPS) * g


def _silu(t):
    return t * jax.nn.sigmoid(t)


def _dot(a, b):
    return jnp.dot(a, b, preferred_element_type=F32)


def _memkv_kernel(mem_ref, g_ref, w_ref, kg_ref, mk_ref, mv_ref):
    mn = _rms(mem_ref[0], g_ref[...], D_MODEL).astype(BF16)
    kv = _dot(mn, w_ref[0])
    for h in range(MEM_HEADS):
        sl = slice(h * MEM_HEAD_DIM, (h + 1) * MEM_HEAD_DIM)
        mk_ref[0, 0, :, sl] = _rms(kv[:, sl], kg_ref[0], MEM_HEAD_DIM).astype(BF16)
    mv_ref[0, 0] = kv[:, MEM_WIDTH:].astype(BF16)


def _memory_kv(mem, mem_norm, w_kv, k_gain):
    bsz, nm, _ = mem.shape
    out = jax.ShapeDtypeStruct((DEPTH, bsz, nm, MEM_WIDTH), BF16)
    return pl.pallas_call(
        _memkv_kernel,
        grid=(DEPTH, bsz),
        in_specs=[
            pl.BlockSpec((1, nm, D_MODEL), lambda l, b: (b, 0, 0)),
            pl.BlockSpec((1, D_MODEL), lambda l, b: (0, 0)),
            pl.BlockSpec((1, D_MODEL, 2 * MEM_WIDTH), lambda l, b: (l, 0, 0)),
            pl.BlockSpec((1, 1, MEM_HEAD_DIM), lambda l, b: (l, 0, 0)),
        ],
        out_specs=[
            pl.BlockSpec((1, 1, nm, MEM_WIDTH), lambda l, b: (l, b, 0, 0)),
            pl.BlockSpec((1, 1, nm, MEM_WIDTH), lambda l, b: (l, b, 0, 0)),
        ],
        out_shape=[out, out],
        compiler_params=_params(("arbitrary", "arbitrary")),
        name="memory_kv",
    )(mem, mem_norm.reshape(1, D_MODEL), w_kv.astype(BF16), k_gain.reshape(DEPTH, 1, MEM_HEAD_DIM))


def _inproj_a_kernel(x_ref, g_ref, wmain_ref, wgate_ref, wmem_ref, cw_ref, cb_ref,
                     mix_ref, gate_ref, mm_ref, tail_ref, *, tiles_per_seq):
    tm = x_ref.shape[0]
    cn = 2 * LANES
    row = lax.broadcasted_iota(jnp.int32, (SUBLANES, cn), 0)

    @pl.when(pl.program_id(0) % tiles_per_seq == 0)
    def _():
        tail_ref[...] = jnp.zeros_like(tail_ref)

    h = _rms(x_ref[...], g_ref[...], D_MODEL).astype(BF16)

    def conv_chunk(c):
        cs = slice(c * cn, (c + 1) * cn)
        u = _dot(h, wmain_ref[:, cs])
        u3 = jnp.concatenate([tail_ref[:, cs], u], axis=0).reshape(tm // SUBLANES + 1, SUBLANES, cn)
        acc = cb_ref[:, cs] + cw_ref[CONV_W - 1:CONV_W, cs] * u3[1:]
        for k in range(1, CONV_W):
            r = pltpu.roll(u3, k, 1)
            shifted = jnp.where(row < k, r[:-1], r[1:])
            acc = acc + cw_ref[CONV_W - 1 - k:CONV_W - k, cs] * shifted
        tail_ref[:, cs] = u[tm - SUBLANES:tm, :]
        scale = M_HEAD_DIM ** -0.5 if c * cn < M_WIDTH else 1.0
        mix_ref[:, cs] = (_silu(acc) * scale).reshape(tm, cn).astype(BF16)

    def value_chunk(c):
        cs = slice(2 * M_WIDTH + c * cn, 2 * M_WIDTH + (c + 1) * cn)
        mix_ref[:, cs] = _dot(h, wmain_ref[:, cs]).astype(BF16)

    def gate_chunk(c):
        o = _dot(h, wmain_ref[:, 3 * M_WIDTH + c * cn:3 * M_WIDTH + (c + 1) * cn])
        z = _dot(h, wmain_ref[:, 4 * M_WIDTH + c * cn:4 * M_WIDTH + (c + 1) * cn])
        mix_ref[:, 3 * M_WIDTH + c * cn:3 * M_WIDTH + (c + 1) * cn] = (
            jax.nn.sigmoid(o) * _silu(z)).astype(BF16)

    def mem_chunk(c):
        cs = slice(c * cn, (c + 1) * cn)
        mm_ref[:, cs] = _dot(h, wmem_ref[:, cs]).astype(BF16)

    for c in range(M_WIDTH // cn):
        conv_chunk(2 * c)
        value_chunk(c)
        conv_chunk(2 * c + 1)
        gate_chunk(c)
        mem_chunk(c)
    gate_ref[...] = _dot(h, wgate_ref[...])


def _inproj_a(x2d, seq, norm_g, w_in, conv_w, conv_b):
    t = x2d.shape[0]
    wmain = w_in[:, :A_MAIN].astype(BF16)
    wgate = jnp.pad(w_in[:, A_MAIN:A_MAIN + 2 * M_HEADS], ((0, 0), (0, LANES - 2 * M_HEADS))).astype(BF16)
    wmem = w_in[:, A_MAIN + 2 * M_HEADS:].astype(BF16)
    tm = TM_A
    return pl.pallas_call(
        functools.partial(_inproj_a_kernel, tiles_per_seq=seq // tm),
        grid=(t // tm,),
        in_specs=[
            pl.BlockSpec((tm, D_MODEL), lambda i: (i, 0)),
            _resident((1, D_MODEL)),
            _resident((D_MODEL, A_MAIN)),
            _resident((D_MODEL, LANES)),
            _resident((D_MODEL, 2 * MEM_WIDTH)),
            _resident((CONV_W, 2 * M_WIDTH)),
            _resident((1, 2 * M_WIDTH)),
        ],
        out_specs=[
            pl.BlockSpec((tm, A_MIX), lambda i: (i, 0)),
            pl.BlockSpec((tm, LANES), lambda i: (i, 0)),
            pl.BlockSpec((tm, 2 * MEM_WIDTH), lambda i: (i, 0)),
        ],
        out_shape=[
            jax.ShapeDtypeStruct((t, A_MIX), BF16),
            jax.ShapeDtypeStruct((t, LANES), F32),
            jax.ShapeDtypeStruct((t, 2 * MEM_WIDTH), BF16),
        ],
        scratch_shapes=[pltpu.VMEM((SUBLANES, 2 * M_WIDTH), F32)],
        compiler_params=_params(("arbitrary",)),
        name="inproj_a",
    )(x2d, norm_g.reshape(1, D_MODEL), wmain, wgate, wmem, conv_w, conv_b.reshape(1, 2 * M_WIDTH))


def _mlstm_kernel(mix_ref, gate_ref, gbias_ref, hg_ref, y_ref, ct_ref, n_ref, m_ref):
    ts, ln, dh = MLSTM_TS, MLSTM_L, M_HEAD_DIM
    heads = range(M_HEADS)

    @pl.when(pl.program_id(1) == 0)
    def _():
        ct_ref[...] = jnp.zeros_like(ct_ref)
        n_ref[...] = jnp.zeros_like(n_ref)
        m_ref[...] = jnp.zeros_like(m_ref)

    def cols(group, h):
        return slice(group * M_WIDTH + h * dh, group * M_WIDTH + (h + 1) * dh)

    def scores(c):
        rs = slice(c * ln, (c + 1) * ln)
        return [lax.dot_general(mix_ref[0, rs, cols(0, h)], mix_ref[0, rs, cols(1, h)], _NT,
                                preferred_element_type=F32) for h in heads]

    qk = scores(0)

    g = gate_ref[0] + gbias_ref[...]
    lsg = jnp.minimum(g, 0.0) - jnp.log1p(jnp.exp(-jnp.abs(g)))
    tri = (lax.broadcasted_iota(jnp.int32, (ln, ln), 0)
           >= lax.broadcasted_iota(jnp.int32, (ln, ln), 1))
    tri_b = jnp.where(tri, 1.0, 0.0).astype(BF16)
    p0 = lsg.astype(BF16)
    r0 = lsg - p0.astype(F32)
    p1 = r0.astype(BF16)
    p2 = (r0 - p1.astype(F32)).astype(BF16)
    pieces = jnp.concatenate([p0, p1, p2], axis=1)
    bcum = []
    for c in range(ts // ln):
        part = _dot(tri_b, pieces[c * ln:(c + 1) * ln])
        bcum.append(part[:, 0:LANES] + part[:, LANES:2 * LANES] + part[:, 2 * LANES:])
    bcum = jnp.concatenate(bcum, axis=0)
    g_t = g.T
    b_t = bcum.T

    m_st = [m_ref[h] for h in heads]
    n_st = [n_ref[h] for h in heads]

    for c in range(ts // ln):
        rs = slice(c * ln, (c + 1) * ln)
        s_l, gi_l, mt_l, kwt_l, ksum_l, gs_l, mnew_l = [], [], [], [], [], [], []
        for h in heads:
            b_col = bcum[rs, M_HEADS + h:M_HEADS + h + 1]
            b_row = b_t[M_HEADS + h:M_HEADS + h + 1, rs]
            i_col = g[rs, h:h + 1]
            i_row = g_t[h:h + 1, rs]
            m_prev = m_st[h]
            d = jnp.where(tri, b_col - b_row + i_row, -jnp.inf)
            inter = b_col + m_prev
            m_t = jnp.maximum(inter, jnp.max(d, axis=1, keepdims=True))
            s_l.append(qk[h] * jnp.exp(d - m_t))
            gi_l.append(jnp.exp(inter - m_t))
            mt_l.append(m_t)
            b_last = b_col[ln - 1:ln, :]
            a_col = b_last - b_col + i_col
            m_new = jnp.maximum(b_last + m_prev, jnp.max(a_col, axis=0, keepdims=True))
            kw = mix_ref[0, rs, cols(1, h)].astype(F32) * jnp.exp(a_col - m_new)
            kwt_l.append(kw.T.astype(BF16))
            ksum_l.append(jnp.sum(kw, axis=0, keepdims=True))
            gs_l.append(jnp.exp(b_last + m_prev - m_new))
            mnew_l.append(m_new)

        sv_l, qc_l, up_l = [], [], []
        for h in heads:
            vb = mix_ref[0, rs, cols(2, h)]
            sv_l.append(_dot(s_l[h].astype(BF16), vb))
            qc_l.append(_dot(mix_ref[0, rs, cols(0, h)], ct_ref[h].astype(BF16)))
            up_l.append(_dot(kwt_l[h], vb))
        if c + 1 < ts // ln:
            qk = scores(c + 1)

        for h in heads:
            qf = mix_ref[0, rs, cols(0, h)].astype(F32)
            num = sv_l[h] + gi_l[h] * qc_l[h]
            den = (jnp.sum(s_l[h], axis=1, keepdims=True)
                   + gi_l[h] * jnp.sum(qf * n_st[h], axis=1, keepdims=True))
            hout = num * (1.0 / jnp.maximum(jnp.abs(den), jnp.exp(-mt_l[h])))
            ct_ref[h] = gs_l[h] * ct_ref[h] + up_l[h]
            n_st[h] = gs_l[h] * n_st[h] + ksum_l[h]
            m_st[h] = mnew_l[h]
            ht = _rms(hout, hg_ref[:, cols(0, h)], dh)
            gate = mix_ref[0, rs, cols(3, h)].astype(F32)
            y_ref[0, rs, cols(0, h)] = (ht * gate).astype(BF16)

    for h in heads:
        n_ref[h] = n_st[h]
        m_ref[h] = m_st[h]


def _mlstm(mix, gates, ig_b, fg_b, h_g):
    bsz, s, _ = mix.shape
    ts = MLSTM_TS
    gbias = jnp.concatenate([ig_b, fg_b, jnp.zeros((LANES - 2 * M_HEADS,), F32)]).reshape(1, LANES)
    return pl.pallas_call(
        _mlstm_kernel,
        grid=(bsz, s // ts),
        in_specs=[
            pl.BlockSpec((1, ts, A_MIX), lambda b, t: (b, t, 0)),
            pl.BlockSpec((1, ts, LANES), lambda b, t: (b, t, 0)),
            _resident((1, LANES)),
            _resident((1, M_WIDTH)),
        ],
        out_specs=pl.BlockSpec((1, ts, M_WIDTH), lambda b, t: (b, t, 0)),
        out_shape=jax.ShapeDtypeStruct((bsz, s, M_WIDTH), BF16),
        scratch_shapes=[
            pltpu.VMEM((M_HEADS, M_HEAD_DIM, M_HEAD_DIM), F32),
            pltpu.VMEM((M_HEADS, 1, M_HEAD_DIM), F32),
            pltpu.VMEM((M_HEADS, 1, 1), F32),
        ],
        compiler_params=_params(("arbitrary", "arbitrary")),
        name="mlstm",
    )(mix, gates, gbias, h_g.reshape(1, M_WIDTH))


def _outproj_kernel(y_ref, mm_ref, mk_ref, mv_ref, qg_ref, w_ref, x_ref, o_ref):
    half = D_MODEL // 2
    heads = range(MEM_HEADS)
    hsl = lambda h: slice(h * MEM_HEAD_DIM, (h + 1) * MEM_HEAD_DIM)
    y = y_ref[0]
    acc_lo = _dot(y, w_ref[0:M_WIDTH, 0:half])
    qs = [_rms(mm_ref[0, :, hsl(h)].astype(F32), qg_ref[...], MEM_HEAD_DIM).astype(BF16) for h in heads]
    sc = [lax.dot_general(qs[h], mk_ref[0, 0, :, hsl(h)], _NT, preferred_element_type=F32) for h in heads]
    acc_hi = _dot(y, w_ref[0:M_WIDTH, half:])
    ps = [jnp.exp2(sc[h] - jnp.max(sc[h], axis=-1, keepdims=True)) for h in heads]
    ymem = []
    for h in heads:
        att = _dot(ps[h].astype(BF16), mv_ref[0, 0, :, hsl(h)]) * (1.0 / jnp.sum(ps[h], axis=-1, keepdims=True))
        z = mm_ref[0, :, MEM_WIDTH + h * MEM_HEAD_DIM:MEM_WIDTH + (h + 1) * MEM_HEAD_DIM].astype(F32)
        ymem.append((att * _silu(z)).astype(BF16))
    ymem = jnp.concatenate(ymem, axis=-1)
    o_ref[0, :, 0:half] = x_ref[0, :, 0:half] + acc_lo + _dot(ymem, w_ref[M_WIDTH:, 0:half])
    o_ref[0, :, half:] = x_ref[0, :, half:] + acc_hi + _dot(ymem, w_ref[M_WIDTH:, half:])


def _outproj(y, mm, mk_all, mv_all, layer, q_gain, w_out, x):
    bsz, s, _ = x.shape
    nm = mk_all.shape[2]
    kv_spec = pl.BlockSpec((1, 1, nm, MEM_WIDTH), lambda b, t: (layer, b, 0, 0))
    tile = lambda w: pl.BlockSpec((1, TM, w), lambda b, t: (b, t, 0))
    qg = (q_gain * (MEM_HEAD_DIM ** -0.5 * LOG2E)).reshape(1, MEM_HEAD_DIM)
    return pl.pallas_call(
        _outproj_kernel,
        grid=(bsz, s // TM),
        in_specs=[
            tile(M_WIDTH), tile(2 * MEM_WIDTH), kv_spec, kv_spec,
            _resident((1, MEM_HEAD_DIM)),
            _resident((M_WIDTH + MEM_WIDTH, D_MODEL)),
            tile(D_MODEL),
        ],
        out_specs=tile(D_MODEL),
        out_shape=jax.ShapeDtypeStruct((bsz, s, D_MODEL), F32),
        compiler_params=_params(("arbitrary", "arbitrary")),
        name="outproj",
    )(y, mm, mk_all, mv_all, qg, w_out.astype(BF16), x)


def _rotate_half(r, cos_t, sin_t):
    half = QK_ROPE // 2
    return r * cos_t + (pltpu.roll(r, half, 1) + pltpu.roll(r, LANES - half, 1)) * sin_t


def _norm_rope_q(t, gain, cos_t, sin_t):
    nope = t[:, :QK_NOPE]
    r = t[:, QK_NOPE:]
    ss = jnp.sum(nope * nope, axis=-1, keepdims=True) + 0.5 * jnp.sum(r * r, axis=-1, keepdims=True)
    rs = lax.rsqrt(ss * (1.0 / QK_HEAD) + EPS)
    r = r * rs * gain[:, QK_NOPE:]
    return nope * rs * gain[:, :QK_NOPE], r * cos_t + pltpu.roll(r, QK_ROPE // 2, 1) * sin_t


def _inproj_b_kernel(x_ref, cos_ref, sin_ref, g_ref, win_ref, qlg_ref, wq_ref, qg_ref,
                     q_ref, zz_ref, mm_ref):
    h = _rms(x_ref[0], g_ref[...], D_MODEL).astype(BF16)
    q_lat = _dot(h, win_ref[:, 0:Q_LORA])
    zz_ref[0, :, 0:B_WIDTH // 2] = _dot(h, win_ref[:, Q_LORA:Q_LORA + B_WIDTH // 2]).astype(BF16)
    ql = _rms(q_lat, qlg_ref[...], Q_LORA).astype(BF16)
    cos_t = cos_ref[0]
    sin_t = sin_ref[0]
    heads = [_dot(ql, wq_ref[:, hd * QK_PAD:(hd + 1) * QK_PAD]) for hd in range(B_HEADS)]
    fill = 2 * LANES
    for lo in range(Q_LORA + B_WIDTH // 2, Q_LORA + B_WIDTH, fill):
        zz_ref[0, :, lo - Q_LORA:lo - Q_LORA + fill] = _dot(h, win_ref[:, lo:lo + fill]).astype(BF16)
    for lo in range(Q_LORA + B_WIDTH, win_ref.shape[1], fill):
        mo = lo - Q_LORA - B_WIDTH
        mm_ref[0, :, mo:mo + fill] = _dot(h, win_ref[:, lo:lo + fill]).astype(BF16)
    for hd in range(B_HEADS):
        nope, rot = _norm_rope_q(heads[hd], qg_ref[...], cos_t, sin_t)
        q_ref[0, hd, :, 0:QK_NOPE] = nope.astype(BF16)
        q_ref[0, hd, :, QK_NOPE:] = rot.astype(BF16)


def _dup_rope(w, heads):
    lead = w.shape[:-1]
    w = w.reshape(lead + (heads, QK_HEAD))
    w = jnp.concatenate([w, w[..., QK_NOPE:]], axis=-1)
    return w.reshape(lead + (heads * QK_PAD,))


def _rope_const():
    half = QK_ROPE // 2
    inv = ROPE_THETA ** (-jnp.arange(0, QK_ROPE, 2, dtype=F32) / QK_ROPE)
    zeros = jnp.zeros((LANES - QK_ROPE,), F32)
    ones = jnp.ones((half,), F32)
    rows = [jnp.concatenate([inv, inv, zeros]),
            jnp.concatenate([ones, ones, zeros]),
            jnp.concatenate([-ones, ones, zeros]),
            jnp.concatenate([0.0 * ones, (jnp.pi / 2) * ones, zeros])]
    return jnp.concatenate([jnp.stack(rows), jnp.zeros((SUBLANES - len(rows), LANES), F32)], axis=0)


def _inproj_b(x, cos_t, sin_t, norm_g, w_in, q_lat_g, w_q_up, q_gain):
    bsz, s, _ = x.shape
    b_in = w_in.shape[1]
    tile = lambda w: pl.BlockSpec((1, TM_B, w), lambda b, t: (b, t, 0))
    qg = _dup_rope(q_gain * (QK_HEAD ** -0.5 * LOG2E), 1).reshape(1, QK_PAD)
    return pl.pallas_call(
        _inproj_b_kernel,
        grid=(bsz, s // TM_B),
        in_specs=[
            tile(D_MODEL), tile(LANES), tile(LANES),
            _resident((1, D_MODEL)),
            _resident((D_MODEL, b_in)),
            _resident((1, Q_LORA)),
            _resident((Q_LORA, B_HEADS * QK_PAD)),
            _resident((1, QK_PAD)),
        ],
        out_specs=[
            pl.BlockSpec((1, B_HEADS, TM_B, QK_PAD), lambda b, t: (b, 0, t, 0)),
            tile(B_WIDTH), tile(2 * MEM_WIDTH),
        ],
        out_shape=[
            jax.ShapeDtypeStruct((bsz, B_HEADS, s, QK_PAD), BF16),
            jax.ShapeDtypeStruct((bsz, s, B_WIDTH), BF16),
            jax.ShapeDtypeStruct((bsz, s, 2 * MEM_WIDTH), BF16),
        ],
        compiler_params=_params(("arbitrary", "arbitrary")),
        name="inproj_b",
    )(x, cos_t, sin_t, norm_g.reshape(1, D_MODEL), w_in.astype(BF16),
      q_lat_g.reshape(1, Q_LORA), _dup_rope(w_q_up, B_HEADS).astype(BF16), qg)


def _sharedkv_kernel(x_ref, pos_ref, tab_ref, g_ref, wa_ref, lg_ref, wb_ref, kg_ref,
                     k_ref, v_ref, cos_ref, sin_ref):
    half = QK_ROPE // 2
    h = _rms(x_ref[0], g_ref[...], D_MODEL).astype(BF16)
    a = _dot(h, wa_ref[...])
    c = _rms(a[:, 0:KV_LORA], lg_ref[...], KV_LORA).astype(BF16)
    k_pe = a[:, KV_LORA:]
    ang = pos_ref[0].astype(F32) * tab_ref[0:1, :] - tab_ref[3:4, :]
    cs = jnp.cos(ang)
    low = lax.broadcasted_iota(jnp.int32, cs.shape, 1) < half
    cos_t = jnp.where(low, cs, pltpu.roll(cs, half, 1)) * tab_ref[1:2, :]
    sin_t = jnp.where(low, pltpu.roll(cs, LANES - half, 1), cs) * tab_ref[2:3, :]
    cos_ref[0] = cos_t
    sin_ref[0] = sin_t
    pe_rot = _rotate_half(k_pe * kg_ref[:, QK_NOPE:], cos_t, sin_t)
    pe_ss = jnp.sum(k_pe * k_pe, axis=-1, keepdims=True)
    for hd in range(B_HEADS):
        kv = _dot(c, wb_ref[:, hd * (QK_NOPE + V_HEAD):(hd + 1) * (QK_NOPE + V_HEAD)])
        nope = kv[:, 0:QK_NOPE]
        rs = lax.rsqrt((jnp.sum(nope * nope, axis=-1, keepdims=True) + pe_ss) * (1.0 / QK_HEAD) + EPS)
        k_ref[0, hd, :, 0:QK_NOPE] = (nope * rs * kg_ref[:, 0:QK_NOPE]).astype(BF16)
        k_ref[0, hd, :, QK_NOPE:] = (pe_rot * rs).astype(BF16)
        v_ref[0, hd] = kv[:, QK_NOPE:].astype(BF16)


def _shared_kv(x, pos3, kv_norm, w_kv_a, kv_lat_norm, w_kv_b, k_gain):
    bsz, s, _ = x.shape
    a_w = KV_LORA + LANES
    tile = lambda w: pl.BlockSpec((1, TM, w), lambda b, t: (b, t, 0))
    table = jax.ShapeDtypeStruct((bsz, s, LANES), F32)
    return pl.pallas_call(
        _sharedkv_kernel,
        grid=(bsz, s // TM),
        in_specs=[
            tile(D_MODEL), tile(1),
            _resident((SUBLANES, LANES)),
            _resident((1, D_MODEL)),
            _resident((D_MODEL, a_w)),
            _resident((1, KV_LORA)),
            _resident((KV_LORA, B_HEADS * (QK_NOPE + V_HEAD))),
            _resident((1, QK_PAD)),
        ],
        out_specs=[
            pl.BlockSpec((1, B_HEADS, TM, QK_PAD), lambda b, t: (b, 0, t, 0)),
            pl.BlockSpec((1, B_HEADS, TM, V_HEAD), lambda b, t: (b, 0, t, 0)),
            tile(LANES), tile(LANES),
        ],
        out_shape=[
            jax.ShapeDtypeStruct((bsz, B_HEADS, s, QK_PAD), BF16),
            jax.ShapeDtypeStruct((bsz, B_HEADS, s, V_HEAD), BF16),
            table, table,
        ],
        compiler_params=_params(("arbitrary", "arbitrary")),
        name="shared_kv",
    )(x, pos3, _rope_const(), kv_norm.reshape(1, D_MODEL),
      jnp.pad(w_kv_a, ((0, 0), (0, a_w - w_kv_a.shape[1]))).astype(BF16),
      kv_lat_norm.reshape(1, KV_LORA), w_kv_b.astype(BF16),
      jnp.pad(k_gain, (0, QK_PAD - QK_HEAD)).reshape(1, QK_PAD))


def _attn_kernel(q_ref, k_ref, v_ref, z_ref, o_ref):
    tq = ATTN_TQ
    s = q_ref.shape[2]
    sh = CHUNK.bit_length() - 1
    rc = lax.broadcasted_iota(jnp.int32, (tq, tq), 0) >> sh
    cc = lax.broadcasted_iota(jnp.int32, (tq, tq), 1) >> sh
    dmask = cc <= rc
    nq = s // tq

    def scores(qi):
        rows = slice(qi * tq, (qi + 1) * tq)
        q = q_ref[0, 0, rows, :]
        sd = lax.dot_general(q, k_ref[0, 0, rows, :], _NT, preferred_element_type=F32)
        so = None
        if qi > 0:
            so = lax.dot_general(q, k_ref[0, 0, 0:qi * tq, :], _NT, preferred_element_type=F32)
        return sd, so

    def finish(qi, pd, po, l):
        rows = slice(qi * tq, (qi + 1) * tq)
        acc = _dot(pd, v_ref[0, 0, rows, :])
        if po is not None:
            acc = acc + _dot(po, v_ref[0, 0, 0:qi * tq, :])
        o_ref[0, rows, :] = (acc * (1.0 / l) * _silu(z_ref[0, rows, :].astype(F32))).astype(BF16)

    nxt = scores(0)
    pending = None
    for qi in range(nq):
        sd, so = nxt
        if qi + 1 < nq:
            nxt = scores(qi + 1)
        if pending is not None:
            finish(*pending)
        sd = jnp.where(dmask, sd, -jnp.inf)
        m = jnp.max(sd, axis=-1, keepdims=True)
        po = None
        if qi > 0:
            m = jnp.maximum(m, jnp.max(so, axis=-1, keepdims=True))
        pd = jnp.exp2(sd - m)
        l = jnp.sum(pd, axis=-1, keepdims=True)
        if qi > 0:
            po = jnp.exp2(so - m)
            l = l + jnp.sum(po, axis=-1, keepdims=True)
            po = po.astype(BF16)
        pending = (qi, pd.astype(BF16), po, l)
    finish(*pending)


def _attention(q, k, v, zz):
    bsz, heads, s, _ = q.shape
    head_spec = lambda w: pl.BlockSpec((1, 1, s, w), lambda b, h: (b, h, 0, 0))
    col_spec = pl.BlockSpec((1, s, V_HEAD), lambda b, h: (b, 0, h))
    return pl.pallas_call(
        _attn_kernel,
        grid=(bsz, heads),
        in_specs=[head_spec(QK_PAD), head_spec(QK_PAD), head_spec(V_HEAD), col_spec],
        out_specs=col_spec,
        out_shape=jax.ShapeDtypeStruct((bsz, s, B_WIDTH), BF16),
        compiler_params=_params(("arbitrary", "arbitrary")),
        name="mla_attention",
    )(q, k, v, zz)


def kernel(x, mem, positions, a_norm, a_w_in, a_conv_w, a_conv_b, a_ig_bias, a_fg_bias, a_h_norm, a_w_out,
           b_norm, b_w_in, b_q_lat_norm, b_w_q_up, b_q_gain, b_w_out,
           kv_norm, w_kv_a, kv_lat_norm, w_kv_b, k_gain,
           mem_norm, mem_w_kv, mem_q_gain, mem_k_gain):
    bsz, s, _ = x.shape
    assert s % MLSTM_TS == 0 and s % TM == 0 and s % TM_A == 0 and s % TM_B == 0 and s % ATTN_TQ == 0
    assert MLSTM_TS % MLSTM_L == 0
    mk_all, mv_all = _memory_kv(mem, mem_norm, mem_w_kv, mem_k_gain)
    pos3 = positions.reshape(bsz, s, 1)
    k_sh = v_sh = cos_t = sin_t = None
    for layer in range(DEPTH):
        if layer < N_A:
            mix, gates, mm = _inproj_a(x.reshape(bsz * s, D_MODEL), s, a_norm[layer], a_w_in[layer],
                                       a_conv_w[layer], a_conv_b[layer])
            y = _mlstm(mix.reshape(bsz, s, A_MIX), gates.reshape(bsz, s, LANES),
                       a_ig_bias[layer], a_fg_bias[layer], a_h_norm[layer])
            x = _outproj(y, mm.reshape(bsz, s, 2 * MEM_WIDTH), mk_all, mv_all, layer,
                         mem_q_gain[layer], a_w_out[layer], x)
        else:
            j = layer - N_A
            if k_sh is None:
                k_sh, v_sh, cos_t, sin_t = _shared_kv(x, pos3, kv_norm, w_kv_a, kv_lat_norm, w_kv_b, k_gain)
            q, zz, mm = _inproj_b(x, cos_t, sin_t, b_norm[j], b_w_in[j], b_q_lat_norm[j], b_w_q_up[j],
                                  b_q_gain[j])
            y = _attention(q, k_sh, v_sh, zz)
            x = _outproj(y, mm, mk_all, mv_all, layer, mem_q_gain[layer], b_w_out[j], x)
    return x
```

```python
import jax
import jax.numpy as jnp
from jax import lax
from jax.experimental import pallas as pl
from jax.experimental.pallas import tpu as pltpu

F32 = jnp.float32
BF16 = jnp.bfloat16

D_MODEL = 1024
DEPTH = 4
N_A = DEPTH // 2
CHUNK = 64
EPS = 1e-6
M_HEADS = 4
M_HEAD_DIM = 256
M_WIDTH = 1024
CONV_W = 4
B_HEADS = 8
QK_NOPE = 128
QK_ROPE = 64
QK_HEAD = 192
QK_PAD = 256
V_HEAD = 128
B_WIDTH = 1024
Q_LORA = 384
KV_LORA = 256
ROPE_THETA = 10000.0
MEM_HEADS = 4
MEM_HEAD_DIM = 128
MEM_WIDTH = 512
A_MAIN = 5 * M_WIDTH
A_MIX = 4 * M_WIDTH
LANES = 128
SUBLANES = 8
LOG2E = 1.4426950408889634

VMEM_LIMIT = 56 * 1024 * 1024

TM = 1024
TM_A = 1024
TM_B = 512
MLSTM_TS = 1024
MLSTM_L = 256
ATTN_TQ = 256
MEM_BATCH = 4
ATTN_HEADS = 2

_NT = (((1,), (1,)), ((), ()))


def _params(sem):
    return pltpu.CompilerParams(dimension_semantics=sem, vmem_limit_bytes=VMEM_LIMIT)


def _resident(shape):
    zeros = (0,) * len(shape)
    return pl.BlockSpec(shape, lambda *_: zeros, pipeline_mode=pl.Buffered(1))


def _layer_resident(shape, layer):
    index = (layer,) + (0,) * len(shape)
    return pl.BlockSpec((None,) + tuple(shape), lambda *_: index, pipeline_mode=pl.Buffered(1))


def _rms(t, g, n):
    ms = jnp.sum(t * t, axis=-1, keepdims=True) * (1.0 / n)
    return t * lax.rsqrt(ms + EPS) * g


def _silu(t):
    return t * jax.nn.sigmoid(t)


def _dot(a, b):
    return jnp.dot(a, b, preferred_element_type=F32)


def _memkv_kernel(mem_ref, g_ref, w_ref, kg_ref, mk_ref, mv_ref):
    nb, nm, _ = mem_ref.shape
    mn = _rms(mem_ref[...].reshape(nb * nm, D_MODEL), g_ref[...], D_MODEL).astype(BF16)
    for layer in range(DEPTH):
        kv = _dot(mn, w_ref[layer])
        mk = [_rms(kv[:, h * MEM_HEAD_DIM:(h + 1) * MEM_HEAD_DIM], kg_ref[layer], MEM_HEAD_DIM).astype(BF16)
              for h in range(MEM_HEADS)]
        mk_ref[layer] = jnp.concatenate(mk, axis=-1).reshape(nb, nm, MEM_WIDTH)
        mv_ref[layer] = kv[:, MEM_WIDTH:].astype(BF16).reshape(nb, nm, MEM_WIDTH)


def _memory_kv(mem, mem_norm, w_kv, k_gain):
    bsz, nm, _ = mem.shape
    nb = MEM_BATCH
    out = jax.ShapeDtypeStruct((DEPTH, bsz, nm, MEM_WIDTH), BF16)
    out_spec = pl.BlockSpec((DEPTH, nb, nm, MEM_WIDTH), lambda i: (0, i, 0, 0))
    return pl.pallas_call(
        _memkv_kernel,
        grid=(bsz // nb,),
        in_specs=[
            pl.BlockSpec((nb, nm, D_MODEL), lambda i: (i, 0, 0)),
            _resident((1, D_MODEL)),
            _resident((DEPTH, D_MODEL, 2 * MEM_WIDTH)),
            _resident((DEPTH, 1, MEM_HEAD_DIM)),
        ],
        out_specs=[out_spec, out_spec],
        out_shape=[out, out],
        compiler_params=_params(("arbitrary",)),
        name="memory_kv",
    )(mem, mem_norm.reshape(1, D_MODEL), w_kv.astype(BF16), k_gain.reshape(DEPTH, 1, MEM_HEAD_DIM))


def _inproj_a_kernel(x_ref, g_ref, wmain_ref, wgate_ref, wmem_ref, cw_ref, cb_ref,
                     mix_ref, gate_ref, mm_ref, tail_ref):
    tm = x_ref.shape[1]
    cn = 2 * LANES
    row = lax.broadcasted_iota(jnp.int32, (SUBLANES, cn), 0)

    @pl.when(pl.program_id(1) == 0)
    def _():
        tail_ref[...] = jnp.zeros_like(tail_ref)

    h = _rms(x_ref[0], g_ref[...], D_MODEL).astype(BF16)

    def conv_chunk(c):
        cs = slice(c * cn, (c + 1) * cn)
        u = _dot(h, wmain_ref[:, cs])
        u3 = jnp.concatenate([tail_ref[:, cs], u], axis=0).reshape(tm // SUBLANES + 1, SUBLANES, cn)
        acc = cb_ref[:, cs] + cw_ref[CONV_W - 1:CONV_W, cs] * u3[1:]
        for k in range(1, CONV_W):
            r = pltpu.roll(u3, k, 1)
            shifted = jnp.where(row < k, r[:-1], r[1:])
            acc = acc + cw_ref[CONV_W - 1 - k:CONV_W - k, cs] * shifted
        tail_ref[:, cs] = u[tm - SUBLANES:tm, :]
        scale = M_HEAD_DIM ** -0.5 if c * cn < M_WIDTH else 1.0
        mix_ref[0, :, cs] = (_silu(acc) * scale).reshape(tm, cn).astype(BF16)

    def value_chunk(c):
        cs = slice(2 * M_WIDTH + c * cn, 2 * M_WIDTH + (c + 1) * cn)
        mix_ref[0, :, cs] = _dot(h, wmain_ref[:, cs]).astype(BF16)

    def gate_chunk(c):
        o = _dot(h, wmain_ref[:, 3 * M_WIDTH + c * cn:3 * M_WIDTH + (c + 1) * cn])
        z = _dot(h, wmain_ref[:, 4 * M_WIDTH + c * cn:4 * M_WIDTH + (c + 1) * cn])
        mix_ref[0, :, 3 * M_WIDTH + c * cn:3 * M_WIDTH + (c + 1) * cn] = (
            jax.nn.sigmoid(o) * _silu(z)).astype(BF16)

    def mem_chunk(c):
        cs = slice(c * cn, (c + 1) * cn)
        mm_ref[0, :, cs] = _dot(h, wmem_ref[:, cs]).astype(BF16)

    for c in range(M_WIDTH // cn):
        conv_chunk(2 * c)
        value_chunk(c)
        conv_chunk(2 * c + 1)
        gate_chunk(c)
        mem_chunk(c)
    gate_ref[0] = _dot(h, wgate_ref[...])


def _inproj_a(x, layer, norm_g, wmain, wgate, wmem, conv_w, conv_b):
    bsz, s, _ = x.shape
    tm = TM_A
    tile = lambda w: pl.BlockSpec((1, tm, w), lambda b, t: (b, t, 0))
    return pl.pallas_call(
        _inproj_a_kernel,
        grid=(bsz, s // tm),
        in_specs=[
            tile(D_MODEL),
            _resident((1, D_MODEL)),
            _layer_resident((D_MODEL, A_MAIN), layer),
            _layer_resident((D_MODEL, LANES), layer),
            _layer_resident((D_MODEL, 2 * MEM_WIDTH), layer),
            _resident((CONV_W, 2 * M_WIDTH)),
            _resident((1, 2 * M_WIDTH)),
        ],
        out_specs=[tile(A_MIX), tile(LANES), tile(2 * MEM_WIDTH)],
        out_shape=[
            jax.ShapeDtypeStruct((bsz, s, A_MIX), BF16),
            jax.ShapeDtypeStruct((bsz, s, LANES), F32),
            jax.ShapeDtypeStruct((bsz, s, 2 * MEM_WIDTH), BF16),
        ],
        scratch_shapes=[pltpu.VMEM((SUBLANES, 2 * M_WIDTH), F32)],
        compiler_params=_params(("arbitrary", "arbitrary")),
        name="inproj_a",
    )(x, norm_g.reshape(1, D_MODEL), wmain, wgate, wmem, conv_w, conv_b.reshape(1, 2 * M_WIDTH))


def _mlstm_kernel(mix_ref, gate_ref, gbias_ref, hg_ref, y_ref, ct_ref, n_ref, m_ref):
    ts, ln, dh = MLSTM_TS, MLSTM_L, M_HEAD_DIM
    heads = range(M_HEADS)

    @pl.when(pl.program_id(1) == 0)
    def _():
        ct_ref[...] = jnp.zeros_like(ct_ref)
        n_ref[...] = jnp.zeros_like(n_ref)
        m_ref[...] = jnp.zeros_like(m_ref)

    def cols(group, h):
        return slice(group * M_WIDTH + h * dh, group * M_WIDTH + (h + 1) * dh)

    def scores(c):
        rs = slice(c * ln, (c + 1) * ln)
        return [lax.dot_general(mix_ref[0, rs, cols(0, h)], mix_ref[0, rs, cols(1, h)], _NT,
                                preferred_element_type=F32) for h in heads]

    qk = scores(0)

    g = gate_ref[0] + gbias_ref[...]
    lsg = jnp.minimum(g, 0.0) - jnp.log1p(jnp.exp(-jnp.abs(g)))
    tri = (lax.broadcasted_iota(jnp.int32, (ln, ln), 0)
           >= lax.broadcasted_iota(jnp.int32, (ln, ln), 1))
    tri_b = jnp.where(tri, 1.0, 0.0).astype(BF16)
    p0 = lsg.astype(BF16)
    r0 = lsg - p0.astype(F32)
    p1 = r0.astype(BF16)
    p2 = (r0 - p1.astype(F32)).astype(BF16)
    pieces = jnp.concatenate([p0, p1, p2], axis=1)
    bcum = []
    for c in range(ts // ln):
        part = _dot(tri_b, pieces[c * ln:(c + 1) * ln])
        bcum.append(part[:, 0:LANES] + part[:, LANES:2 * LANES] + part[:, 2 * LANES:])
    bcum = jnp.concatenate(bcum, axis=0)
    g_t = g.T
    b_t = bcum.T

    m_st = [m_ref[h] for h in heads]
    n_st = [n_ref[h] for h in heads]

    for c in range(ts // ln):
        rs = slice(c * ln, (c + 1) * ln)
        s_l, gi_l, mt_l, kwt_l, ksum_l, gs_l, mnew_l = [], [], [], [], [], [], []
        for h in heads:
            b_col = bcum[rs, M_HEADS + h:M_HEADS + h + 1]
            b_row = b_t[M_HEADS + h:M_HEADS + h + 1, rs]
            i_col = g[rs, h:h + 1]
            i_row = g_t[h:h + 1, rs]
            m_prev = m_st[h]
            d = jnp.where(tri, b_col - b_row + i_row, -jnp.inf)
            inter = b_col + m_prev
            m_t = jnp.maximum(inter, jnp.max(d, axis=1, keepdims=True))
            s_l.append(qk[h] * jnp.exp(d - m_t))
            gi_l.append(jnp.exp(inter - m_t))
            mt_l.append(m_t)
            b_last = b_col[ln - 1:ln, :]
            a_col = b_last - b_col + i_col
            m_new = jnp.maximum(b_last + m_prev, jnp.max(a_col, axis=0, keepdims=True))
            kw = mix_ref[0, rs, cols(1, h)].astype(F32) * jnp.exp(a_col - m_new)
            kwt_l.append(kw.T.astype(BF16))
            ksum_l.append(jnp.sum(kw, axis=0, keepdims=True))
            gs_l.append(jnp.exp(b_last + m_prev - m_new))
            mnew_l.append(m_new)

        sv_l, qc_l, up_l = [], [], []
        for h in heads:
            vb = mix_ref[0, rs, cols(2, h)]
            sv_l.append(_dot(s_l[h].astype(BF16), vb))
            qc_l.append(_dot(mix_ref[0, rs, cols(0, h)], ct_ref[h].astype(BF16)))
            up_l.append(_dot(kwt_l[h], vb))
        if c + 1 < ts // ln:
            qk = scores(c + 1)

        for h in heads:
            qf = mix_ref[0, rs, cols(0, h)].astype(F32)
            num = sv_l[h] + gi_l[h] * qc_l[h]
            den = (jnp.sum(s_l[h], axis=1, keepdims=True)
                   + gi_l[h] * jnp.sum(qf * n_st[h], axis=1, keepdims=True))
            hout = num * (1.0 / jnp.maximum(jnp.abs(den), jnp.exp(-mt_l[h])))
            ct_ref[h] = gs_l[h] * ct_ref[h] + up_l[h]
            n_st[h] = gs_l[h] * n_st[h] + ksum_l[h]
            m_st[h] = mnew_l[h]
            ht = _rms(hout, hg_ref[:, cols(0, h)], dh)
            gate = mix_ref[0, rs, cols(3, h)].astype(F32)
            y_ref[0, rs, cols(0, h)] = (ht * gate).astype(BF16)

    for h in heads:
        n_ref[h] = n_st[h]
        m_ref[h] = m_st[h]


def _mlstm(mix, gates, ig_b, fg_b, h_g):
    bsz, s, _ = mix.shape
    ts = MLSTM_TS
    gbias = jnp.concatenate([ig_b, fg_b, jnp.zeros((LANES - 2 * M_HEADS,), F32)]).reshape(1, LANES)
    return pl.pallas_call(
        _mlstm_kernel,
        grid=(bsz, s // ts),
        in_specs=[
            pl.BlockSpec((1, ts, A_MIX), lambda b, t: (b, t, 0)),
            pl.BlockSpec((1, ts, LANES), lambda b, t: (b, t, 0)),
            _resident((1, LANES)),
            _resident((1, M_WIDTH)),
        ],
        out_specs=pl.BlockSpec((1, ts, M_WIDTH), lambda b, t: (b, t, 0)),
        out_shape=jax.ShapeDtypeStruct((bsz, s, M_WIDTH), BF16),
        scratch_shapes=[
            pltpu.VMEM((M_HEADS, M_HEAD_DIM, M_HEAD_DIM), F32),
            pltpu.VMEM((M_HEADS, 1, M_HEAD_DIM), F32),
            pltpu.VMEM((M_HEADS, 1, 1), F32),
        ],
        compiler_params=_params(("arbitrary", "arbitrary")),
        name="mlstm",
    )(mix, gates, gbias, h_g.reshape(1, M_WIDTH))


def _outproj_kernel(y_ref, mm_ref, mk_ref, mv_ref, qg_ref, w_ref, x_ref, o_ref):
    half = D_MODEL // 2
    heads = range(MEM_HEADS)
    hsl = lambda h: slice(h * MEM_HEAD_DIM, (h + 1) * MEM_HEAD_DIM)
    y = y_ref[0]
    acc_lo = _dot(y, w_ref[0:M_WIDTH, 0:half])
    qs = [_rms(mm_ref[0, :, hsl(h)].astype(F32), qg_ref[...], MEM_HEAD_DIM).astype(BF16) for h in heads]
    sc = [lax.dot_general(qs[h], mk_ref[0, 0, :, hsl(h)], _NT, preferred_element_type=F32) for h in heads]
    acc_hi = _dot(y, w_ref[0:M_WIDTH, half:])
    ps = [jnp.exp2(sc[h] - jnp.max(sc[h], axis=-1, keepdims=True)) for h in heads]
    ymem = []
    for h in heads:
        att = _dot(ps[h].astype(BF16), mv_ref[0, 0, :, hsl(h)]) * (1.0 / jnp.sum(ps[h], axis=-1, keepdims=True))
        z = mm_ref[0, :, MEM_WIDTH + h * MEM_HEAD_DIM:MEM_WIDTH + (h + 1) * MEM_HEAD_DIM].astype(F32)
        ymem.append((att * _silu(z)).astype(BF16))
    ymem = jnp.concatenate(ymem, axis=-1)
    o_ref[0, :, 0:half] = x_ref[0, :, 0:half] + acc_lo + _dot(ymem, w_ref[M_WIDTH:, 0:half])
    o_ref[0, :, half:] = x_ref[0, :, half:] + acc_hi + _dot(ymem, w_ref[M_WIDTH:, half:])


def _outproj(y, mm, mk_all, mv_all, layer, q_gain, w_out, w_layer, x):
    bsz, s, _ = x.shape
    nm = mk_all.shape[2]
    kv_spec = pl.BlockSpec((1, 1, nm, MEM_WIDTH), lambda b, t: (layer, b, 0, 0))
    tile = lambda w: pl.BlockSpec((1, TM, w), lambda b, t: (b, t, 0))
    qg = (q_gain * (MEM_HEAD_DIM ** -0.5 * LOG2E)).reshape(1, MEM_HEAD_DIM)
    return pl.pallas_call(
        _outproj_kernel,
        grid=(bsz, s // TM),
        in_specs=[
            tile(M_WIDTH), tile(2 * MEM_WIDTH), kv_spec, kv_spec,
            _resident((1, MEM_HEAD_DIM)),
            _layer_resident((M_WIDTH + MEM_WIDTH, D_MODEL), w_layer),
            tile(D_MODEL),
        ],
        out_specs=tile(D_MODEL),
        out_shape=jax.ShapeDtypeStruct((bsz, s, D_MODEL), F32),
        compiler_params=_params(("arbitrary", "arbitrary")),
        name="outproj",
    )(y, mm, mk_all, mv_all, qg, w_out, x)


def _rotate_half(r, cos_t, sin_t):
    half = QK_ROPE // 2
    return r * cos_t + (pltpu.roll(r, half, 1) + pltpu.roll(r, LANES - half, 1)) * sin_t


def _norm_rope_q(t, gain, cos_t, sin_t):
    nope = t[:, :QK_NOPE]
    r = t[:, QK_NOPE:]
    ss = jnp.sum(nope * nope, axis=-1, keepdims=True) + 0.5 * jnp.sum(r * r, axis=-1, keepdims=True)
    rs = lax.rsqrt(ss * (1.0 / QK_HEAD) + EPS)
    r = r * rs * gain[:, QK_NOPE:]
    return nope * rs * gain[:, :QK_NOPE], r * cos_t + pltpu.roll(r, QK_ROPE // 2, 1) * sin_t


def _inproj_b_kernel(x_ref, cos_ref, sin_ref, g_ref, win_ref, qlg_ref, wq_ref, qg_ref,
                     q_ref, zz_ref, mm_ref):
    h = _rms(x_ref[0], g_ref[...], D_MODEL).astype(BF16)
    q_lat = _dot(h, win_ref[:, 0:Q_LORA])
    zz_ref[0, :, 0:B_WIDTH // 2] = _dot(h, win_ref[:, Q_LORA:Q_LORA + B_WIDTH // 2]).astype(BF16)
    ql = _rms(q_lat, qlg_ref[...], Q_LORA).astype(BF16)
    cos_t = cos_ref[0]
    sin_t = sin_ref[0]
    heads = [_dot(ql, wq_ref[:, hd * QK_PAD:(hd + 1) * QK_PAD]) for hd in range(B_HEADS)]
    fill = 2 * LANES
    for lo in range(Q_LORA + B_WIDTH // 2, Q_LORA + B_WIDTH, fill):
        zz_ref[0, :, lo - Q_LORA:lo - Q_LORA + fill] = _dot(h, win_ref[:, lo:lo + fill]).astype(BF16)
    for lo in range(Q_LORA + B_WIDTH, win_ref.shape[1], fill):
        mo = lo - Q_LORA - B_WIDTH
        mm_ref[0, :, mo:mo + fill] = _dot(h, win_ref[:, lo:lo + fill]).astype(BF16)
    for hd in range(B_HEADS):
        nope, rot = _norm_rope_q(heads[hd], qg_ref[...], cos_t, sin_t)
        q_ref[0, hd, :, 0:QK_NOPE] = nope.astype(BF16)
        q_ref[0, hd, :, QK_NOPE:] = rot.astype(BF16)


def _dup_rope(w, heads):
    lead = w.shape[:-1]
    w = w.reshape(lead + (heads, QK_HEAD))
    w = jnp.concatenate([w, w[..., QK_NOPE:]], axis=-1)
    return w.reshape(lead + (heads * QK_PAD,))


def _rope_const():
    half = QK_ROPE // 2
    inv = ROPE_THETA ** (-jnp.arange(0, QK_ROPE, 2, dtype=F32) / QK_ROPE)
    zeros = jnp.zeros((LANES - QK_ROPE,), F32)
    ones = jnp.ones((half,), F32)
    rows = [jnp.concatenate([inv, inv, zeros]),
            jnp.concatenate([ones, ones, zeros]),
            jnp.concatenate([-ones, ones, zeros]),
            jnp.concatenate([0.0 * ones, (jnp.pi / 2) * ones, zeros])]
    return jnp.concatenate([jnp.stack(rows), jnp.zeros((SUBLANES - len(rows), LANES), F32)], axis=0)


def _inproj_b(x, cos_t, sin_t, layer, norm_g, w_in, q_lat_g, w_q_up, q_gain):
    bsz, s, _ = x.shape
    b_in = w_in.shape[-1]
    tile = lambda w: pl.BlockSpec((1, TM_B, w), lambda b, t: (b, t, 0))
    qg = _dup_rope(q_gain * (QK_HEAD ** -0.5 * LOG2E), 1).reshape(1, QK_PAD)
    return pl.pallas_call(
        _inproj_b_kernel,
        grid=(bsz, s // TM_B),
        in_specs=[
            tile(D_MODEL), tile(LANES), tile(LANES),
            _resident((1, D_MODEL)),
            _layer_resident((D_MODEL, b_in), layer),
            _resident((1, Q_LORA)),
            _layer_resident((Q_LORA, B_HEADS * QK_PAD), layer),
            _resident((1, QK_PAD)),
        ],
        out_specs=[
            pl.BlockSpec((1, B_HEADS, TM_B, QK_PAD), lambda b, t: (b, 0, t, 0)),
            tile(B_WIDTH), tile(2 * MEM_WIDTH),
        ],
        out_shape=[
            jax.ShapeDtypeStruct((bsz, B_HEADS, s, QK_PAD), BF16),
            jax.ShapeDtypeStruct((bsz, s, B_WIDTH), BF16),
            jax.ShapeDtypeStruct((bsz, s, 2 * MEM_WIDTH), BF16),
        ],
        compiler_params=_params(("arbitrary", "arbitrary")),
        name="inproj_b",
    )(x, cos_t, sin_t, norm_g.reshape(1, D_MODEL), w_in, q_lat_g.reshape(1, Q_LORA), w_q_up, qg)


def _sharedkv_kernel(x_ref, pos_ref, tab_ref, g_ref, wa_ref, lg_ref, wb_ref, kg_ref,
                     k_ref, v_ref, cos_ref, sin_ref):
    half = QK_ROPE // 2
    h = _rms(x_ref[0], g_ref[...], D_MODEL).astype(BF16)
    a = _dot(h, wa_ref[...])
    c = _rms(a[:, 0:KV_LORA], lg_ref[...], KV_LORA).astype(BF16)
    k_pe = a[:, KV_LORA:]
    ang = pos_ref[0].astype(F32) * tab_ref[0:1, :] - tab_ref[3:4, :]
    cs = jnp.cos(ang)
    low = lax.broadcasted_iota(jnp.int32, cs.shape, 1) < half
    cos_t = jnp.where(low, cs, pltpu.roll(cs, half, 1)) * tab_ref[1:2, :]
    sin_t = jnp.where(low, pltpu.roll(cs, LANES - half, 1), cs) * tab_ref[2:3, :]
    cos_ref[0] = cos_t
    sin_ref[0] = sin_t
    pe_rot = _rotate_half(k_pe * kg_ref[:, QK_NOPE:], cos_t, sin_t)
    pe_ss = jnp.sum(k_pe * k_pe, axis=-1, keepdims=True)
    for hd in range(B_HEADS):
        kv = _dot(c, wb_ref[:, hd * (QK_NOPE + V_HEAD):(hd + 1) * (QK_NOPE + V_HEAD)])
        nope = kv[:, 0:QK_NOPE]
        rs = lax.rsqrt((jnp.sum(nope * nope, axis=-1, keepdims=True) + pe_ss) * (1.0 / QK_HEAD) + EPS)
        k_ref[0, hd, :, 0:QK_NOPE] = (nope * rs * kg_ref[:, 0:QK_NOPE]).astype(BF16)
        k_ref[0, hd, :, QK_NOPE:] = (pe_rot * rs).astype(BF16)
        v_ref[0, hd] = kv[:, QK_NOPE:].astype(BF16)


def _shared_kv(x, pos3, kv_norm, w_kv_a, kv_lat_norm, w_kv_b, k_gain):
    bsz, s, _ = x.shape
    a_w = KV_LORA + LANES
    tile = lambda w: pl.BlockSpec((1, TM, w), lambda b, t: (b, t, 0))
    table = jax.ShapeDtypeStruct((bsz, s, LANES), F32)
    return pl.pallas_call(
        _sharedkv_kernel,
        grid=(bsz, s // TM),
        in_specs=[
            tile(D_MODEL), tile(1),
            _resident((SUBLANES, LANES)),
            _resident((1, D_MODEL)),
            _resident((D_MODEL, a_w)),
            _resident((1, KV_LORA)),
            _resident((KV_LORA, B_HEADS * (QK_NOPE + V_HEAD))),
            _resident((1, QK_PAD)),
        ],
        out_specs=[
            pl.BlockSpec((1, B_HEADS, TM, QK_PAD), lambda b, t: (b, 0, t, 0)),
            pl.BlockSpec((1, B_HEADS, TM, V_HEAD), lambda b, t: (b, 0, t, 0)),
            tile(LANES), tile(LANES),
        ],
        out_shape=[
            jax.ShapeDtypeStruct((bsz, B_HEADS, s, QK_PAD), BF16),
            jax.ShapeDtypeStruct((bsz, B_HEADS, s, V_HEAD), BF16),
            table, table,
        ],
        compiler_params=_params(("arbitrary", "arbitrary")),
        name="shared_kv",
    )(x, pos3, _rope_const(), kv_norm.reshape(1, D_MODEL),
      jnp.pad(w_kv_a, ((0, 0), (0, a_w - w_kv_a.shape[1]))).astype(BF16),
      kv_lat_norm.reshape(1, KV_LORA), w_kv_b.astype(BF16),
      jnp.pad(k_gain, (0, QK_PAD - QK_HEAD)).reshape(1, QK_PAD))


def _attn_kernel(q_ref, k_ref, v_ref, z_ref, o_ref):
    tq = ATTN_TQ
    nh = q_ref.shape[1]
    s = q_ref.shape[2]
    sh = CHUNK.bit_length() - 1
    rc = lax.broadcasted_iota(jnp.int32, (tq, tq), 0) >> sh
    cc = lax.broadcasted_iota(jnp.int32, (tq, tq), 1) >> sh
    dmask = cc <= rc
    nq = s // tq

    def scores(hh, qi):
        rows = slice(qi * tq, (qi + 1) * tq)
        q = q_ref[0, hh, rows, :]
        sd = lax.dot_general(q, k_ref[0, hh, rows, :], _NT, preferred_element_type=F32)
        so = None
        if qi > 0:
            so = lax.dot_general(q, k_ref[0, hh, 0:qi * tq, :], _NT, preferred_element_type=F32)
        return sd, so

    def finish(hh, qi, pd, po, l):
        rows = slice(qi * tq, (qi + 1) * tq)
        hc = slice(hh * V_HEAD, (hh + 1) * V_HEAD)
        acc = _dot(pd, v_ref[0, hh, rows, :])
        if po is not None:
            acc = acc + _dot(po, v_ref[0, hh, 0:qi * tq, :])
        o_ref[0, rows, hc] = (acc * (1.0 / l) * _silu(z_ref[0, rows, hc].astype(F32))).astype(BF16)

    nxt = [scores(hh, 0) for hh in range(nh)]
    pending = [None] * nh
    for qi in range(nq):
        for hh in range(nh):
            sd, so = nxt[hh]
            if qi + 1 < nq:
                nxt[hh] = scores(hh, qi + 1)
            if pending[hh] is not None:
                finish(*pending[hh])
            sd = jnp.where(dmask, sd, -jnp.inf)
            m = jnp.max(sd, axis=-1, keepdims=True)
            po = None
            if qi > 0:
                m = jnp.maximum(m, jnp.max(so, axis=-1, keepdims=True))
            pd = jnp.exp2(sd - m)
            l = jnp.sum(pd, axis=-1, keepdims=True)
            if qi > 0:
                po = jnp.exp2(so - m)
                l = l + jnp.sum(po, axis=-1, keepdims=True)
                po = po.astype(BF16)
            pending[hh] = (hh, qi, pd.astype(BF16), po, l)
    for hh in range(nh):
        finish(*pending[hh])


def _attention(q, k, v, zz):
    bsz, heads, s, _ = q.shape
    nh = ATTN_HEADS
    head_spec = lambda w: pl.BlockSpec((1, nh, s, w), lambda b, h: (b, h, 0, 0))
    col_spec = pl.BlockSpec((1, s, nh * V_HEAD), lambda b, h: (b, 0, h))
    return pl.pallas_call(
        _attn_kernel,
        grid=(bsz, heads // nh),
        in_specs=[head_spec(QK_PAD), head_spec(QK_PAD), head_spec(V_HEAD), col_spec],
        out_specs=col_spec,
        out_shape=jax.ShapeDtypeStruct((bsz, s, B_WIDTH), BF16),
        compiler_params=_params(("arbitrary", "arbitrary")),
        name="mla_attention",
    )(q, k, v, zz)


def kernel(x, mem, positions, a_norm, a_w_in, a_conv_w, a_conv_b, a_ig_bias, a_fg_bias, a_h_norm, a_w_out,
           b_norm, b_w_in, b_q_lat_norm, b_w_q_up, b_q_gain, b_w_out,
           kv_norm, w_kv_a, kv_lat_norm, w_kv_b, k_gain,
           mem_norm, mem_w_kv, mem_q_gain, mem_k_gain):
    bsz, s, _ = x.shape
    assert s % MLSTM_TS == 0 and s % TM == 0 and s % TM_A == 0 and s % TM_B == 0 and s % ATTN_TQ == 0
    assert MLSTM_TS % MLSTM_L == 0 and bsz % MEM_BATCH == 0
    mk_all, mv_all = _memory_kv(mem, mem_norm, mem_w_kv, mem_k_gain)
    pos3 = positions.reshape(bsz, s, 1)
    a_wmain = a_w_in[:, :, :A_MAIN].astype(BF16)
    a_wgate = jnp.pad(a_w_in[:, :, A_MAIN:A_MAIN + 2 * M_HEADS],
                      ((0, 0), (0, 0), (0, LANES - 2 * M_HEADS))).astype(BF16)
    a_wmem = a_w_in[:, :, A_MAIN + 2 * M_HEADS:].astype(BF16)
    a_wout = a_w_out.astype(BF16)
    b_win = b_w_in.astype(BF16)
    b_wq = _dup_rope(b_w_q_up, B_HEADS).astype(BF16)
    b_wout = b_w_out.astype(BF16)
    k_sh = v_sh = cos_t = sin_t = None
    for layer in range(DEPTH):
        if layer < N_A:
            mix, gates, mm = _inproj_a(x, layer, a_norm[layer], a_wmain, a_wgate, a_wmem,
                                       a_conv_w[layer], a_conv_b[layer])
            y = _mlstm(mix, gates, a_ig_bias[layer], a_fg_bias[layer], a_h_norm[layer])
            x = _outproj(y, mm, mk_all, mv_all, layer, mem_q_gain[layer], a_wout, layer, x)
        else:
            j = layer - N_A
            if k_sh is None:
                k_sh, v_sh, cos_t, sin_t = _shared_kv(x, pos3, kv_norm, w_kv_a, kv_lat_norm, w_kv_b, k_gain)
            q, zz, mm = _inproj_b(x, cos_t, sin_t, j, b_norm[j], b_win, b_q_lat_norm[j], b_wq, b_q_gain[j])
            y = _attention(q, k_sh, v_sh, zz)
            x = _outproj(y, mm, mk_all, mv_all, layer, mem_q_gain[layer], b_wout, j, x)
    return x
```

```python
import jax
import jax.numpy as jnp
from jax import lax
from jax.experimental import pallas as pl
from jax.experimental.pallas import tpu as pltpu

F32 = jnp.float32
BF16 = jnp.bfloat16

D_MODEL = 1024
DEPTH = 4
N_A = DEPTH // 2
CHUNK = 64
EPS = 1e-6
M_HEADS = 4
M_HEAD_DIM = 256
M_WIDTH = 1024
CONV_W = 4
B_HEADS = 8
QK_NOPE = 128
QK_ROPE = 64
QK_HEAD = 192
QK_PAD = 256
V_HEAD = 128
B_WIDTH = 1024
Q_LORA = 384
KV_LORA = 256
ROPE_THETA = 10000.0
MEM_HEADS = 4
MEM_HEAD_DIM = 128
MEM_WIDTH = 512
A_MAIN = 5 * M_WIDTH
A_MIX = 4 * M_WIDTH
LANES = 128
SUBLANES = 8
LOG2E = 1.4426950408889634

VMEM_LIMIT = 56 * 1024 * 1024

TM = 1024
TM_A = 1024
TM_B = 512
MLSTM_TS = 1024
MLSTM_L = 256
ATTN_TQ = 256
MEM_BATCH = 4
ATTN_HEADS = 2

_NT = (((1,), (1,)), ((), ()))


def _params(sem):
    return pltpu.CompilerParams(dimension_semantics=sem, vmem_limit_bytes=VMEM_LIMIT)


def _resident(shape):
    zeros = (0,) * len(shape)
    return pl.BlockSpec(shape, lambda *_: zeros, pipeline_mode=pl.Buffered(1))


def _layer_resident(shape, layer):
    index = (layer,) + (0,) * len(shape)
    return pl.BlockSpec((None,) + tuple(shape), lambda *_: index, pipeline_mode=pl.Buffered(1))


def _rms(t, g, n):
    ms = jnp.sum(t * t, axis=-1, keepdims=True) * (1.0 / n)
    return t * lax.rsqrt(ms + EPS) * g


def _silu(t):
    return t * jax.nn.sigmoid(t)


def _dot(a, b):
    return jnp.dot(a, b, preferred_element_type=F32)


def _memkv_kernel(mem_ref, g_ref, w_ref, kg_ref, mk_ref, mv_ref):
    nb, nm, _ = mem_ref.shape
    mn = _rms(mem_ref[...].reshape(nb * nm, D_MODEL), g_ref[...], D_MODEL).astype(BF16)
    for layer in range(DEPTH):
        kv = _dot(mn, w_ref[layer])
        mk = [_rms(kv[:, h * MEM_HEAD_DIM:(h + 1) * MEM_HEAD_DIM], kg_ref[layer], MEM_HEAD_DIM).astype(BF16)
              for h in range(MEM_HEADS)]
        mk_ref[layer] = jnp.concatenate(mk, axis=-1).reshape(nb, nm, MEM_WIDTH)
        mv_ref[layer] = kv[:, MEM_WIDTH:].astype(BF16).reshape(nb, nm, MEM_WIDTH)


def _memory_kv(mem, mem_norm, w_kv, k_gain):
    bsz, nm, _ = mem.shape
    nb = MEM_BATCH
    out = jax.ShapeDtypeStruct((DEPTH, bsz, nm, MEM_WIDTH), BF16)
    out_spec = pl.BlockSpec((DEPTH, nb, nm, MEM_WIDTH), lambda i: (0, i, 0, 0))
    return pl.pallas_call(
        _memkv_kernel,
        grid=(bsz // nb,),
        in_specs=[
            pl.BlockSpec((nb, nm, D_MODEL), lambda i: (i, 0, 0)),
            _resident((1, D_MODEL)),
            _resident((DEPTH, D_MODEL, 2 * MEM_WIDTH)),
            _resident((DEPTH, 1, MEM_HEAD_DIM)),
        ],
        out_specs=[out_spec, out_spec],
        out_shape=[out, out],
        compiler_params=_params(("arbitrary",)),
        name="memory_kv",
    )(mem, mem_norm.reshape(1, D_MODEL), w_kv.astype(BF16), k_gain.reshape(DEPTH, 1, MEM_HEAD_DIM))


def _inproj_a_kernel(x_ref, g_ref, wmain_ref, wgate_ref, wmem_ref, cw_ref, cb_ref,
                     mix_ref, gate_ref, mm_ref, tail_ref):
    tm = x_ref.shape[1]
    cn = 2 * LANES
    row = lax.broadcasted_iota(jnp.int32, (SUBLANES, cn), 0)

    @pl.when(pl.program_id(1) == 0)
    def _():
        tail_ref[...] = jnp.zeros_like(tail_ref)

    h = _rms(x_ref[0], g_ref[...], D_MODEL).astype(BF16)

    def conv_chunk(c):
        cs = slice(c * cn, (c + 1) * cn)
        u = _dot(h, wmain_ref[:, cs])
        u3 = jnp.concatenate([tail_ref[:, cs], u], axis=0).reshape(tm // SUBLANES + 1, SUBLANES, cn)
        acc = cb_ref[:, cs] + cw_ref[CONV_W - 1:CONV_W, cs] * u3[1:]
        for k in range(1, CONV_W):
            r = pltpu.roll(u3, k, 1)
            shifted = jnp.where(row < k, r[:-1], r[1:])
            acc = acc + cw_ref[CONV_W - 1 - k:CONV_W - k, cs] * shifted
        tail_ref[:, cs] = u[tm - SUBLANES:tm, :]
        scale = M_HEAD_DIM ** -0.5 if c * cn < M_WIDTH else 1.0
        mix_ref[0, :, cs] = (_silu(acc) * scale).reshape(tm, cn).astype(BF16)

    def value_chunk(c):
        cs = slice(2 * M_WIDTH + c * cn, 2 * M_WIDTH + (c + 1) * cn)
        mix_ref[0, :, cs] = _dot(h, wmain_ref[:, cs]).astype(BF16)

    def gate_chunk(c):
        o = _dot(h, wmain_ref[:, 3 * M_WIDTH + c * cn:3 * M_WIDTH + (c + 1) * cn])
        z = _dot(h, wmain_ref[:, 4 * M_WIDTH + c * cn:4 * M_WIDTH + (c + 1) * cn])
        mix_ref[0, :, 3 * M_WIDTH + c * cn:3 * M_WIDTH + (c + 1) * cn] = (
            jax.nn.sigmoid(o) * _silu(z)).astype(BF16)

    def mem_chunk(c):
        cs = slice(c * cn, (c + 1) * cn)
        mm_ref[0, :, cs] = _dot(h, wmem_ref[:, cs]).astype(BF16)

    for c in range(M_WIDTH // cn):
        conv_chunk(2 * c)
        value_chunk(c)
        conv_chunk(2 * c + 1)
        gate_chunk(c)
        mem_chunk(c)
    gate_ref[0] = _dot(h, wgate_ref[...])


def _inproj_a(x, layer, norm_g, wmain, wgate, wmem, conv_w, conv_b):
    bsz, s, _ = x.shape
    tm = TM_A
    tile = lambda w: pl.BlockSpec((1, tm, w), lambda b, t: (b, t, 0))
    return pl.pallas_call(
        _inproj_a_kernel,
        grid=(bsz, s // tm),
        in_specs=[
            tile(D_MODEL),
            _resident((1, D_MODEL)),
            _layer_resident((D_MODEL, A_MAIN), layer),
            _layer_resident((D_MODEL, LANES), layer),
            _layer_resident((D_MODEL, 2 * MEM_WIDTH), layer),
            _resident((CONV_W, 2 * M_WIDTH)),
            _resident((1, 2 * M_WIDTH)),
        ],
        out_specs=[tile(A_MIX), tile(LANES), tile(2 * MEM_WIDTH)],
        out_shape=[
            jax.ShapeDtypeStruct((bsz, s, A_MIX), BF16),
            jax.ShapeDtypeStruct((bsz, s, LANES), F32),
            jax.ShapeDtypeStruct((bsz, s, 2 * MEM_WIDTH), BF16),
        ],
        scratch_shapes=[pltpu.VMEM((SUBLANES, 2 * M_WIDTH), F32)],
        compiler_params=_params(("arbitrary", "arbitrary")),
        name="inproj_a",
    )(x, norm_g.reshape(1, D_MODEL), wmain, wgate, wmem, conv_w, conv_b.reshape(1, 2 * M_WIDTH))


def _mlstm_kernel(mix_ref, gate_ref, gbias_ref, hg_ref, y_ref, ct_ref, n_ref, m_ref):
    ts, ln, dh = MLSTM_TS, MLSTM_L, M_HEAD_DIM
    heads = range(M_HEADS)

    @pl.when(pl.program_id(1) == 0)
    def _():
        ct_ref[...] = jnp.zeros_like(ct_ref)
        n_ref[...] = jnp.zeros_like(n_ref)
        m_ref[...] = jnp.zeros_like(m_ref)

    def cols(group, h):
        return slice(group * M_WIDTH + h * dh, group * M_WIDTH + (h + 1) * dh)

    def scores(c):
        rs = slice(c * ln, (c + 1) * ln)
        return [lax.dot_general(mix_ref[0, rs, cols(0, h)], mix_ref[0, rs, cols(1, h)], _NT,
                                preferred_element_type=F32) for h in heads]

    qk = scores(0)

    g = gate_ref[0] + gbias_ref[...]
    lsg = jnp.minimum(g, 0.0) - jnp.log1p(jnp.exp(-jnp.abs(g)))
    tri = (lax.broadcasted_iota(jnp.int32, (ln, ln), 0)
           >= lax.broadcasted_iota(jnp.int32, (ln, ln), 1))
    tri_b = jnp.where(tri, 1.0, 0.0).astype(BF16)
    p0 = lsg.astype(BF16)
    r0 = lsg - p0.astype(F32)
    p1 = r0.astype(BF16)
    p2 = (r0 - p1.astype(F32)).astype(BF16)
    pieces = jnp.concatenate([p0, p1, p2], axis=1)
    bcum = []
    for c in range(ts // ln):
        part = _dot(tri_b, pieces[c * ln:(c + 1) * ln])
        bcum.append(part[:, 0:LANES] + part[:, LANES:2 * LANES] + part[:, 2 * LANES:])
    bcum = jnp.concatenate(bcum, axis=0)
    g_t = g.T
    b_t = bcum.T

    m_st = [m_ref[h] for h in heads]
    n_st = [n_ref[h] for h in heads]

    for c in range(ts // ln):
        rs = slice(c * ln, (c + 1) * ln)
        s_l, gi_l, mt_l, kwt_l, ksum_l, gs_l, mnew_l = [], [], [], [], [], [], []
        for h in heads:
            b_col = bcum[rs, M_HEADS + h:M_HEADS + h + 1]
            b_row = b_t[M_HEADS + h:M_HEADS + h + 1, rs]
            i_col = g[rs, h:h + 1]
            i_row = g_t[h:h + 1, rs]
            m_prev = m_st[h]
            d = jnp.where(tri, b_col - b_row + i_row, -jnp.inf)
            inter = b_col + m_prev
            m_t = jnp.maximum(inter, jnp.max(d, axis=1, keepdims=True))
            s_l.append(qk[h] * jnp.exp(d - m_t))
            gi_l.append(jnp.exp(inter - m_t))
            mt_l.append(m_t)
            b_last = b_col[ln - 1:ln, :]
            a_col = b_last - b_col + i_col
            m_new = jnp.maximum(b_last + m_prev, jnp.max(a_col, axis=0, keepdims=True))
            kw = mix_ref[0, rs, cols(1, h)].astype(F32) * jnp.exp(a_col - m_new)
            kwt_l.append(kw.T.astype(BF16))
            ksum_l.append(jnp.sum(kw, axis=0, keepdims=True))
            gs_l.append(jnp.exp(b_last + m_prev - m_new))
            mnew_l.append(m_new)

        sv_l, qc_l, up_l = [], [], []
        for h in heads:
            vb = mix_ref[0, rs, cols(2, h)]
            sv_l.append(_dot(s_l[h].astype(BF16), vb))
            qc_l.append(_dot(mix_ref[0, rs, cols(0, h)], ct_ref[h].astype(BF16)))
            up_l.append(_dot(kwt_l[h], vb))
        if c + 1 < ts // ln:
            qk = scores(c + 1)

        for h in heads:
            qf = mix_ref[0, rs, cols(0, h)].astype(F32)
            num = sv_l[h] + gi_l[h] * qc_l[h]
            den = (jnp.sum(s_l[h], axis=1, keepdims=True)
                   + gi_l[h] * jnp.sum(qf * n_st[h], axis=1, keepdims=True))
            hout = num * (1.0 / jnp.maximum(jnp.abs(den), jnp.exp(-mt_l[h])))
            ct_ref[h] = gs_l[h] * ct_ref[h] + up_l[h]
            n_st[h] = gs_l[h] * n_st[h] + ksum_l[h]
            m_st[h] = mnew_l[h]
            ht = _rms(hout, hg_ref[:, cols(0, h)], dh)
            gate = mix_ref[0, rs, cols(3, h)].astype(F32)
            y_ref[0, rs, cols(0, h)] = (ht * gate).astype(BF16)

    for h in heads:
        n_ref[h] = n_st[h]
        m_ref[h] = m_st[h]


def _mlstm(mix, gates, ig_b, fg_b, h_g):
    bsz, s, _ = mix.shape
    ts = MLSTM_TS
    gbias = jnp.concatenate([ig_b, fg_b, jnp.zeros((LANES - 2 * M_HEADS,), F32)]).reshape(1, LANES)
    return pl.pallas_call(
        _mlstm_kernel,
        grid=(bsz, s // ts),
        in_specs=[
            pl.BlockSpec((1, ts, A_MIX), lambda b, t: (b, t, 0)),
            pl.BlockSpec((1, ts, LANES), lambda b, t: (b, t, 0)),
            _resident((1, LANES)),
            _resident((1, M_WIDTH)),
        ],
        out_specs=pl.BlockSpec((1, ts, M_WIDTH), lambda b, t: (b, t, 0)),
        out_shape=jax.ShapeDtypeStruct((bsz, s, M_WIDTH), BF16),
        scratch_shapes=[
            pltpu.VMEM((M_HEADS, M_HEAD_DIM, M_HEAD_DIM), F32),
            pltpu.VMEM((M_HEADS, 1, M_HEAD_DIM), F32),
            pltpu.VMEM((M_HEADS, 1, 1), F32),
        ],
        compiler_params=_params(("arbitrary", "arbitrary")),
        name="mlstm",
    )(mix, gates, gbias, h_g.reshape(1, M_WIDTH))


def _outproj_kernel(y_ref, mm_ref, mk_ref, mv_ref, qg_ref, w_ref, x_ref, o_ref):
    half = D_MODEL // 2
    heads = range(MEM_HEADS)
    hsl = lambda h: slice(h * MEM_HEAD_DIM, (h + 1) * MEM_HEAD_DIM)
    y = y_ref[0]
    acc_lo = _dot(y, w_ref[0:M_WIDTH, 0:half])
    qs = [_rms(mm_ref[0, :, hsl(h)].astype(F32), qg_ref[...], MEM_HEAD_DIM).astype(BF16) for h in heads]
    sc = [lax.dot_general(qs[h], mk_ref[0, 0, :, hsl(h)], _NT, preferred_element_type=F32) for h in heads]
    acc_hi = _dot(y, w_ref[0:M_WIDTH, half:])
    ps = [jnp.exp2(sc[h] - jnp.max(sc[h], axis=-1, keepdims=True)) for h in heads]
    ymem = []
    for h in heads:
        att = _dot(ps[h].astype(BF16), mv_ref[0, 0, :, hsl(h)]) * (1.0 / jnp.sum(ps[h], axis=-1, keepdims=True))
        z = mm_ref[0, :, MEM_WIDTH + h * MEM_HEAD_DIM:MEM_WIDTH + (h + 1) * MEM_HEAD_DIM].astype(F32)
        ymem.append((att * _silu(z)).astype(BF16))
    ymem = jnp.concatenate(ymem, axis=-1)
    o_ref[0, :, 0:half] = x_ref[0, :, 0:half] + acc_lo + _dot(ymem, w_ref[M_WIDTH:, 0:half])
    o_ref[0, :, half:] = x_ref[0, :, half:] + acc_hi + _dot(ymem, w_ref[M_WIDTH:, half:])


def _outproj(y, mm, mk_all, mv_all, layer, q_gain, w_out, w_layer, x):
    bsz, s, _ = x.shape
    nm = mk_all.shape[2]
    kv_spec = pl.BlockSpec((1, 1, nm, MEM_WIDTH), lambda b, t: (layer, b, 0, 0))
    tile = lambda w: pl.BlockSpec((1, TM, w), lambda b, t: (b, t, 0))
    qg = (q_gain * (MEM_HEAD_DIM ** -0.5 * LOG2E)).reshape(1, MEM_HEAD_DIM)
    return pl.pallas_call(
        _outproj_kernel,
        grid=(bsz, s // TM),
        in_specs=[
            tile(M_WIDTH), tile(2 * MEM_WIDTH), kv_spec, kv_spec,
            _resident((1, MEM_HEAD_DIM)),
            _layer_resident((M_WIDTH + MEM_WIDTH, D_MODEL), w_layer),
            tile(D_MODEL),
        ],
        out_specs=tile(D_MODEL),
        out_shape=jax.ShapeDtypeStruct((bsz, s, D_MODEL), F32),
        compiler_params=_params(("arbitrary", "arbitrary")),
        name="outproj",
    )(y, mm, mk_all, mv_all, qg, w_out, x)


def _rotate_half(r, cos_t, sin_t):
    half = QK_ROPE // 2
    return r * cos_t + (pltpu.roll(r, half, 1) + pltpu.roll(r, LANES - half, 1)) * sin_t


def _norm_rope_q(t, gain, cos_t, sin_t):
    nope = t[:, :QK_NOPE]
    r = t[:, QK_NOPE:]
    ss = jnp.sum(nope * nope, axis=-1, keepdims=True) + 0.5 * jnp.sum(r * r, axis=-1, keepdims=True)
    rs = lax.rsqrt(ss * (1.0 / QK_HEAD) + EPS)
    r = r * rs * gain[:, QK_NOPE:]
    return nope * rs * gain[:, :QK_NOPE], r * cos_t + pltpu.roll(r, QK_ROPE // 2, 1) * sin_t


def _inproj_b_kernel(x_ref, cos_ref, sin_ref, g_ref, win_ref, qlg_ref, wq_ref, qg_ref,
                     q_ref, zz_ref, mm_ref):
    h = _rms(x_ref[0], g_ref[...], D_MODEL).astype(BF16)
    q_lat = _dot(h, win_ref[:, 0:Q_LORA])
    zz_ref[0, :, 0:B_WIDTH // 2] = _dot(h, win_ref[:, Q_LORA:Q_LORA + B_WIDTH // 2]).astype(BF16)
    ql = _rms(q_lat, qlg_ref[...], Q_LORA).astype(BF16)
    cos_t = cos_ref[0]
    sin_t = sin_ref[0]
    heads = [_dot(ql, wq_ref[:, hd * QK_PAD:(hd + 1) * QK_PAD]) for hd in range(B_HEADS)]
    fill = 2 * LANES
    for lo in range(Q_LORA + B_WIDTH // 2, Q_LORA + B_WIDTH, fill):
        zz_ref[0, :, lo - Q_LORA:lo - Q_LORA + fill] = _dot(h, win_ref[:, lo:lo + fill]).astype(BF16)
    for lo in range(Q_LORA + B_WIDTH, win_ref.shape[1], fill):
        mo = lo - Q_LORA - B_WIDTH
        mm_ref[0, :, mo:mo + fill] = _dot(h, win_ref[:, lo:lo + fill]).astype(BF16)
    for hd in range(B_HEADS):
        nope, rot = _norm_rope_q(heads[hd], qg_ref[...], cos_t, sin_t)
        q_ref[0, hd, :, 0:QK_NOPE] = nope.astype(BF16)
        q_ref[0, hd, :, QK_NOPE:] = rot.astype(BF16)


def _dup_rope(w, heads):
    lead = w.shape[:-1]
    w = w.reshape(lead + (heads, QK_HEAD))
    w = jnp.concatenate([w, w[..., QK_NOPE:]], axis=-1)
    return w.reshape(lead + (heads * QK_PAD,))


def _rope_const():
    half = QK_ROPE // 2
    inv = ROPE_THETA ** (-jnp.arange(0, QK_ROPE, 2, dtype=F32) / QK_ROPE)
    zeros = jnp.zeros((LANES - QK_ROPE,), F32)
    ones = jnp.ones((half,), F32)
    rows = [jnp.concatenate([inv, inv, zeros]),
            jnp.concatenate([ones, ones, zeros]),
            jnp.concatenate([-ones, ones, zeros]),
            jnp.concatenate([0.0 * ones, (jnp.pi / 2) * ones, zeros])]
    return jnp.concatenate([jnp.stack(rows), jnp.zeros((SUBLANES - len(rows), LANES), F32)], axis=0)


def _inproj_b(x, cos_t, sin_t, layer, norm_g, w_in, q_lat_g, w_q_up, q_gain):
    bsz, s, _ = x.shape
    b_in = w_in.shape[-1]
    tile = lambda w: pl.BlockSpec((1, TM_B, w), lambda b, t: (b, t, 0))
    qg = _dup_rope(q_gain * (QK_HEAD ** -0.5 * LOG2E), 1).reshape(1, QK_PAD)
    return pl.pallas_call(
        _inproj_b_kernel,
        grid=(bsz, s // TM_B),
        in_specs=[
            tile(D_MODEL), tile(LANES), tile(LANES),
            _resident((1, D_MODEL)),
            _layer_resident((D_MODEL, b_in), layer),
            _resident((1, Q_LORA)),
            _layer_resident((Q_LORA, B_HEADS * QK_PAD), layer),
            _resident((1, QK_PAD)),
        ],
        out_specs=[
            pl.BlockSpec((1, B_HEADS, TM_B, QK_PAD), lambda b, t: (b, 0, t, 0)),
            tile(B_WIDTH), tile(2 * MEM_WIDTH),
        ],
        out_shape=[
            jax.ShapeDtypeStruct((bsz, B_HEADS, s, QK_PAD), BF16),
            jax.ShapeDtypeStruct((bsz, s, B_WIDTH), BF16),
            jax.ShapeDtypeStruct((bsz, s, 2 * MEM_WIDTH), BF16),
        ],
        compiler_params=_params(("arbitrary", "arbitrary")),
        name="inproj_b",
    )(x, cos_t, sin_t, norm_g.reshape(1, D_MODEL), w_in, q_lat_g.reshape(1, Q_LORA), w_q_up, qg)


def _sharedkv_kernel(x_ref, pos_ref, tab_ref, g_ref, wa_ref, lg_ref, wb_ref, kg_ref,
                     k_ref, v_ref, cos_ref, sin_ref):
    half = QK_ROPE // 2
    h = _rms(x_ref[0], g_ref[...], D_MODEL).astype(BF16)
    a = _dot(h, wa_ref[...])
    c = _rms(a[:, 0:KV_LORA], lg_ref[...], KV_LORA).astype(BF16)
    k_pe = a[:, KV_LORA:]
    ang = pos_ref[0].astype(F32) * tab_ref[0:1, :] - tab_ref[3:4, :]
    cs = jnp.cos(ang)
    low = lax.broadcasted_iota(jnp.int32, cs.shape, 1) < half
    cos_t = jnp.where(low, cs, pltpu.roll(cs, half, 1)) * tab_ref[1:2, :]
    sin_t = jnp.where(low, pltpu.roll(cs, LANES - half, 1), cs) * tab_ref[2:3, :]
    cos_ref[0] = cos_t
    sin_ref[0] = sin_t
    pe_rot = _rotate_half(k_pe * kg_ref[:, QK_NOPE:], cos_t, sin_t)
    pe_ss = jnp.sum(k_pe * k_pe, axis=-1, keepdims=True)
    for hd in range(B_HEADS):
        kv = _dot(c, wb_ref[:, hd * (QK_NOPE + V_HEAD):(hd + 1) * (QK_NOPE + V_HEAD)])
        nope = kv[:, 0:QK_NOPE]
        rs = lax.rsqrt((jnp.sum(nope * nope, axis=-1, keepdims=True) + pe_ss) * (1.0 / QK_HEAD) + EPS)
        k_ref[0, hd, :, 0:QK_NOPE] = (nope * rs * kg_ref[:, 0:QK_NOPE]).astype(BF16)
        k_ref[0, hd, :, QK_NOPE:] = (pe_rot * rs).astype(BF16)
        v_ref[0, hd] = kv[:, QK_NOPE:].astype(BF16)


def _shared_kv(x, pos3, kv_norm, w_kv_a, kv_lat_norm, w_kv_b, k_gain):
    bsz, s, _ = x.shape
    a_w = KV_LORA + LANES
    tile = lambda w: pl.BlockSpec((1, TM, w), lambda b, t: (b, t, 0))
    table = jax.ShapeDtypeStruct((bsz, s, LANES), F32)
    return pl.pallas_call(
        _sharedkv_kernel,
        grid=(bsz, s // TM),
        in_specs=[
            tile(D_MODEL), tile(1),
            _resident((SUBLANES, LANES)),
            _resident((1, D_MODEL)),
            _resident((D_MODEL, a_w)),
            _resident((1, KV_LORA)),
            _resident((KV_LORA, B_HEADS * (QK_NOPE + V_HEAD))),
            _resident((1, QK_PAD)),
        ],
        out_specs=[
            pl.BlockSpec((1, B_HEADS, TM, QK_PAD), lambda b, t: (b, 0, t, 0)),
            pl.BlockSpec((1, B_HEADS, TM, V_HEAD), lambda b, t: (b, 0, t, 0)),
            tile(LANES), tile(LANES),
        ],
        out_shape=[
            jax.ShapeDtypeStruct((bsz, B_HEADS, s, QK_PAD), BF16),
            jax.ShapeDtypeStruct((bsz, B_HEADS, s, V_HEAD), BF16),
            table, table,
        ],
        compiler_params=_params(("arbitrary", "arbitrary")),
        name="shared_kv",
    )(x, pos3, _rope_const(), kv_norm.reshape(1, D_MODEL),
      jnp.pad(w_kv_a, ((0, 0), (0, a_w - w_kv_a.shape[1]))).astype(BF16),
      kv_lat_norm.reshape(1, KV_LORA), w_kv_b.astype(BF16),
      jnp.pad(k_gain, (0, QK_PAD - QK_HEAD)).reshape(1, QK_PAD))


def _attn_kernel(q_ref, k_ref, v_ref, z_ref, o_ref):
    tq = ATTN_TQ
    nh = q_ref.shape[1]
    s = q_ref.shape[2]
    sh = CHUNK.bit_length() - 1
    rc = lax.broadcasted_iota(jnp.int32, (tq, tq), 0) >> sh
    cc = lax.broadcasted_iota(jnp.int32, (tq, tq), 1) >> sh
    dmask = cc <= rc
    nq = s // tq

    def scores(hh, qi):
        rows = slice(qi * tq, (qi + 1) * tq)
        q = q_ref[0, hh, rows, :]
        sd = lax.dot_general(q, k_ref[0, hh, rows, :], _NT, preferred_element_type=F32)
        so = None
        if qi > 0:
            so = lax.dot_general(q, k_ref[0, hh, 0:qi * tq, :], _NT, preferred_element_type=F32)
        return sd, so

    def finish(hh, qi, pd, po, l):
        rows = slice(qi * tq, (qi + 1) * tq)
        hc = slice(hh * V_HEAD, (hh + 1) * V_HEAD)
        acc = _dot(pd, v_ref[0, hh, rows, :])
        if po is not None:
            acc = acc + _dot(po, v_ref[0, hh, 0:qi * tq, :])
        o_ref[0, rows, hc] = (acc * (1.0 / l) * _silu(z_ref[0, rows, hc].astype(F32))).astype(BF16)

    nxt = [scores(hh, 0) for hh in range(nh)]
    pending = [None] * nh
    for qi in range(nq):
        for hh in range(nh):
            sd, so = nxt[hh]
            if qi + 1 < nq:
                nxt[hh] = scores(hh, qi + 1)
            if pending[hh] is not None:
                finish(*pending[hh])
            sd = jnp.where(dmask, sd, -jnp.inf)
            m = jnp.max(sd, axis=-1, keepdims=True)
            po = None
            if qi > 0:
                m = jnp.maximum(m, jnp.max(so, axis=-1, keepdims=True))
            pd = jnp.exp2(sd - m)
            l = jnp.sum(pd, axis=-1, keepdims=True)
            if qi > 0:
                po = jnp.exp2(so - m)
                l = l + jnp.sum(po, axis=-1, keepdims=True)
                po = po.astype(BF16)
            pending[hh] = (hh, qi, pd.astype(BF16), po, l)
    for hh in range(nh):
        finish(*pending[hh])


def _attention(q, k, v, zz):
    bsz, heads, s, _ = q.shape
    nh = ATTN_HEADS
    head_spec = lambda w: pl.BlockSpec((1, nh, s, w), lambda b, h: (b, h, 0, 0))
    col_spec = pl.BlockSpec((1, s, nh * V_HEAD), lambda b, h: (b, 0, h))
    return pl.pallas_call(
        _attn_kernel,
        grid=(bsz, heads // nh),
        in_specs=[head_spec(QK_PAD), head_spec(QK_PAD), head_spec(V_HEAD), col_spec],
        out_specs=col_spec,
        out_shape=jax.ShapeDtypeStruct((bsz, s, B_WIDTH), BF16),
        compiler_params=_params(("arbitrary", "arbitrary")),
        name="mla_attention",
    )(q, k, v, zz)


def kernel(x, mem, positions, a_norm, a_w_in, a_conv_w, a_conv_b, a_ig_bias, a_fg_bias, a_h_norm, a_w_out,
           b_norm, b_w_in, b_q_lat_norm, b_w_q_up, b_q_gain, b_w_out,
           kv_norm, w_kv_a, kv_lat_norm, w_kv_b, k_gain,
           mem_norm, mem_w_kv, mem_q_gain, mem_k_gain):
    bsz, s, _ = x.shape
    assert s % MLSTM_TS == 0 and s % TM == 0 and s % TM_A == 0 and s % TM_B == 0 and s % ATTN_TQ == 0
    assert MLSTM_TS % MLSTM_L == 0 and bsz % MEM_BATCH == 0
    mk_all, mv_all = _memory_kv(mem, mem_norm, mem_w_kv, mem_k_gain)
    pos3 = positions.reshape(bsz, s, 1)
    a_wmain = a_w_in.astype(BF16)
    a_wgate = jnp.pad(a_wmain[:, :, A_MAIN:A_MAIN + 2 * M_HEADS],
                      ((0, 0), (0, 0), (0, LANES - 2 * M_HEADS)))
    a_wmem = a_wmain[:, :, A_MAIN + 2 * M_HEADS:]
    a_wout = a_w_out.astype(BF16)
    b_win = b_w_in.astype(BF16)
    b_wq = _dup_rope(b_w_q_up, B_HEADS).astype(BF16)
    b_wout = b_w_out.astype(BF16)
    k_sh = v_sh = cos_t = sin_t = None
    for layer in range(DEPTH):
        if layer < N_A:
            mix, gates, mm = _inproj_a(x, layer, a_norm[layer], a_wmain, a_wgate, a_wmem,
                                       a_conv_w[layer], a_conv_b[layer])
            y = _mlstm(mix, gates, a_ig_bias[layer], a_fg_bias[layer], a_h_norm[layer])
            x = _outproj(y, mm, mk_all, mv_all, layer, mem_q_gain[layer], a_wout, layer, x)
        else:
            j = layer - N_A
            if k_sh is None:
                k_sh, v_sh, cos_t, sin_t = _shared_kv(x, pos3, kv_norm, w_kv_a, kv_lat_norm, w_kv_b, k_gain)
            q, zz, mm = _inproj_b(x, cos_t, sin_t, j, b_norm[j], b_win, b_q_lat_norm[j], b_wq, b_q_gain[j])
            y = _attention(q, k_sh, v_sh, zz)
            x = _outproj(y, mm, mk_all, mv_all, layer, mem_q_gain[layer], b_wout, j, x)
    return x
```

```python
import jax
import jax.numpy as jnp
from jax import lax
from jax.experimental import pallas as pl
from jax.experimental.pallas import tpu as pltpu

F32 = jnp.float32
BF16 = jnp.bfloat16

D_MODEL = 1024
DEPTH = 4
N_A = DEPTH // 2
CHUNK = 64
EPS = 1e-6
M_HEADS = 4
M_HEAD_DIM = 256
M_WIDTH = 1024
CONV_W = 4
B_HEADS = 8
QK_NOPE = 128
QK_ROPE = 64
QK_HEAD = 192
QK_PAD = 256
V_HEAD = 128
B_WIDTH = 1024
Q_LORA = 384
KV_LORA = 256
ROPE_THETA = 10000.0
MEM_HEADS = 4
MEM_HEAD_DIM = 128
MEM_WIDTH = 512
A_MAIN = 5 * M_WIDTH
A_MIX = 4 * M_WIDTH
LANES = 128
SUBLANES = 8
LOG2E = 1.4426950408889634

VMEM_LIMIT = 56 * 1024 * 1024

TM = 1024
TM_A = 1024
TM_B = 512
MLSTM_TS = 1024
MLSTM_L = 256
ATTN_TQ = 256
MEM_BATCH = 4
ATTN_HEADS = 4

_NT = (((1,), (1,)), ((), ()))


def _params(sem):
    return pltpu.CompilerParams(dimension_semantics=sem, vmem_limit_bytes=VMEM_LIMIT)


def _resident(shape):
    zeros = (0,) * len(shape)
    return pl.BlockSpec(shape, lambda *_: zeros, pipeline_mode=pl.Buffered(1))


def _layer_resident(shape, layer):
    index = (layer,) + (0,) * len(shape)
    return pl.BlockSpec((None,) + tuple(shape), lambda *_: index, pipeline_mode=pl.Buffered(1))


def _rms(t, g, n):
    ms = jnp.sum(t * t, axis=-1, keepdims=True) * (1.0 / n)
    return t * lax.rsqrt(ms + EPS) * g


def _silu(t):
    return t * jax.nn.sigmoid(t)


def _dot(a, b):
    return jnp.dot(a, b, preferred_element_type=F32)


def _memkv_kernel(mem_ref, g_ref, w_ref, kg_ref, mk_ref, mv_ref):
    nb, nm, _ = mem_ref.shape
    mn = _rms(mem_ref[...].reshape(nb * nm, D_MODEL), g_ref[...], D_MODEL).astype(BF16)
    for layer in range(DEPTH):
        kv = _dot(mn, w_ref[layer])
        mk = [_rms(kv[:, h * MEM_HEAD_DIM:(h + 1) * MEM_HEAD_DIM], kg_ref[layer], MEM_HEAD_DIM).astype(BF16)
              for h in range(MEM_HEADS)]
        mk_ref[layer] = jnp.concatenate(mk, axis=-1).reshape(nb, nm, MEM_WIDTH)
        mv_ref[layer] = kv[:, MEM_WIDTH:].astype(BF16).reshape(nb, nm, MEM_WIDTH)


def _memory_kv(mem, mem_norm, w_kv, k_gain):
    bsz, nm, _ = mem.shape
    nb = MEM_BATCH
    out = jax.ShapeDtypeStruct((DEPTH, bsz, nm, MEM_WIDTH), BF16)
    out_spec = pl.BlockSpec((DEPTH, nb, nm, MEM_WIDTH), lambda i: (0, i, 0, 0))
    return pl.pallas_call(
        _memkv_kernel,
        grid=(bsz // nb,),
        in_specs=[
            pl.BlockSpec((nb, nm, D_MODEL), lambda i: (i, 0, 0)),
            _resident((1, D_MODEL)),
            _resident((DEPTH, D_MODEL, 2 * MEM_WIDTH)),
            _resident((DEPTH, 1, MEM_HEAD_DIM)),
        ],
        out_specs=[out_spec, out_spec],
        out_shape=[out, out],
        compiler_params=_params(("arbitrary",)),
        name="memory_kv",
    )(mem, mem_norm.reshape(1, D_MODEL), w_kv.astype(BF16), k_gain.reshape(DEPTH, 1, MEM_HEAD_DIM))


def _inproj_a_kernel(x_ref, g_ref, wmain_ref, wgate_ref, wmem_ref, cw_ref, cb_ref,
                     mix_ref, gate_ref, mm_ref, tail_ref):
    tm = x_ref.shape[1]
    cn = 2 * LANES
    row = lax.broadcasted_iota(jnp.int32, (SUBLANES, cn), 0)

    @pl.when(pl.program_id(1) == 0)
    def _():
        tail_ref[...] = jnp.zeros_like(tail_ref)

    h = _rms(x_ref[0], g_ref[...], D_MODEL).astype(BF16)

    def conv_chunk(c):
        cs = slice(c * cn, (c + 1) * cn)
        u = _dot(h, wmain_ref[:, cs])
        u3 = jnp.concatenate([tail_ref[:, cs], u], axis=0).reshape(tm // SUBLANES + 1, SUBLANES, cn)
        acc = cb_ref[:, cs] + cw_ref[CONV_W - 1:CONV_W, cs] * u3[1:]
        for k in range(1, CONV_W):
            r = pltpu.roll(u3, k, 1)
            shifted = jnp.where(row < k, r[:-1], r[1:])
            acc = acc + cw_ref[CONV_W - 1 - k:CONV_W - k, cs] * shifted
        tail_ref[:, cs] = u[tm - SUBLANES:tm, :]
        scale = M_HEAD_DIM ** -0.5 if c * cn < M_WIDTH else 1.0
        mix_ref[0, :, cs] = (_silu(acc) * scale).reshape(tm, cn).astype(BF16)

    def value_chunk(c):
        cs = slice(2 * M_WIDTH + c * cn, 2 * M_WIDTH + (c + 1) * cn)
        mix_ref[0, :, cs] = _dot(h, wmain_ref[:, cs]).astype(BF16)

    def gate_chunk(c):
        o = _dot(h, wmain_ref[:, 3 * M_WIDTH + c * cn:3 * M_WIDTH + (c + 1) * cn])
        z = _dot(h, wmain_ref[:, 4 * M_WIDTH + c * cn:4 * M_WIDTH + (c + 1) * cn])
        mix_ref[0, :, 3 * M_WIDTH + c * cn:3 * M_WIDTH + (c + 1) * cn] = (
            jax.nn.sigmoid(o) * _silu(z)).astype(BF16)

    def mem_chunk(c):
        cs = slice(c * cn, (c + 1) * cn)
        mm_ref[0, :, cs] = _dot(h, wmem_ref[:, cs]).astype(BF16)

    for c in range(M_WIDTH // cn):
        conv_chunk(2 * c)
        value_chunk(c)
        conv_chunk(2 * c + 1)
        gate_chunk(c)
        mem_chunk(c)
    gate_ref[0] = _dot(h, wgate_ref[...])


def _inproj_a(x, layer, norm_g, wmain, wgate, wmem, conv_w, conv_b):
    bsz, s, _ = x.shape
    tm = TM_A
    tile = lambda w: pl.BlockSpec((1, tm, w), lambda b, t: (b, t, 0))
    return pl.pallas_call(
        _inproj_a_kernel,
        grid=(bsz, s // tm),
        in_specs=[
            tile(D_MODEL),
            _resident((1, D_MODEL)),
            _layer_resident((D_MODEL, A_MAIN), layer),
            _layer_resident((D_MODEL, LANES), layer),
            _layer_resident((D_MODEL, 2 * MEM_WIDTH), layer),
            _resident((CONV_W, 2 * M_WIDTH)),
            _resident((1, 2 * M_WIDTH)),
        ],
        out_specs=[tile(A_MIX), tile(LANES), tile(2 * MEM_WIDTH)],
        out_shape=[
            jax.ShapeDtypeStruct((bsz, s, A_MIX), BF16),
            jax.ShapeDtypeStruct((bsz, s, LANES), F32),
            jax.ShapeDtypeStruct((bsz, s, 2 * MEM_WIDTH), BF16),
        ],
        scratch_shapes=[pltpu.VMEM((SUBLANES, 2 * M_WIDTH), F32)],
        compiler_params=_params(("arbitrary", "arbitrary")),
        name="inproj_a",
    )(x, norm_g.reshape(1, D_MODEL), wmain, wgate, wmem, conv_w, conv_b.reshape(1, 2 * M_WIDTH))


def _mlstm_kernel(mix_ref, gate_ref, gbias_ref, hg_ref, y_ref, ct_ref, n_ref, m_ref):
    ts, ln, dh = MLSTM_TS, MLSTM_L, M_HEAD_DIM
    heads = range(M_HEADS)

    @pl.when(pl.program_id(1) == 0)
    def _():
        ct_ref[...] = jnp.zeros_like(ct_ref)
        n_ref[...] = jnp.zeros_like(n_ref)
        m_ref[...] = jnp.zeros_like(m_ref)

    def cols(group, h):
        return slice(group * M_WIDTH + h * dh, group * M_WIDTH + (h + 1) * dh)

    def scores(c):
        rs = slice(c * ln, (c + 1) * ln)
        return [lax.dot_general(mix_ref[0, rs, cols(0, h)], mix_ref[0, rs, cols(1, h)], _NT,
                                preferred_element_type=F32) for h in heads]

    qk = scores(0)

    g = gate_ref[0] + gbias_ref[...]
    lsg = jnp.minimum(g, 0.0) - jnp.log1p(jnp.exp(-jnp.abs(g)))
    tri = (lax.broadcasted_iota(jnp.int32, (ln, ln), 0)
           >= lax.broadcasted_iota(jnp.int32, (ln, ln), 1))
    tri_b = jnp.where(tri, 1.0, 0.0).astype(BF16)
    p0 = lsg.astype(BF16)
    r0 = lsg - p0.astype(F32)
    p1 = r0.astype(BF16)
    p2 = (r0 - p1.astype(F32)).astype(BF16)
    pieces = jnp.concatenate([p0, p1, p2], axis=1)
    bcum = []
    for c in range(ts // ln):
        part = _dot(tri_b, pieces[c * ln:(c + 1) * ln])
        bcum.append(part[:, 0:LANES] + part[:, LANES:2 * LANES] + part[:, 2 * LANES:])
    bcum = jnp.concatenate(bcum, axis=0)
    g_t = g.T
    b_t = bcum.T

    m_st = [m_ref[h] for h in heads]
    n_st = [n_ref[h] for h in heads]

    for c in range(ts // ln):
        rs = slice(c * ln, (c + 1) * ln)
        s_l, gi_l, mt_l, kwt_l, ksum_l, gs_l, mnew_l = [], [], [], [], [], [], []
        for h in heads:
            b_col = bcum[rs, M_HEADS + h:M_HEADS + h + 1]
            b_row = b_t[M_HEADS + h:M_HEADS + h + 1, rs]
            i_col = g[rs, h:h + 1]
            i_row = g_t[h:h + 1, rs]
            m_prev = m_st[h]
            d = jnp.where(tri, b_col - b_row + i_row, -jnp.inf)
            inter = b_col + m_prev
            m_t = jnp.maximum(inter, jnp.max(d, axis=1, keepdims=True))
            s_l.append(qk[h] * jnp.exp(d - m_t))
            gi_l.append(jnp.exp(inter - m_t))
            mt_l.append(m_t)
            b_last = b_col[ln - 1:ln, :]
            a_col = b_last - b_col + i_col
            m_new = jnp.maximum(b_last + m_prev, jnp.max(a_col, axis=0, keepdims=True))
            kw = mix_ref[0, rs, cols(1, h)].astype(F32) * jnp.exp(a_col - m_new)
            kwt_l.append(kw.T.astype(BF16))
            ksum_l.append(jnp.sum(kw, axis=0, keepdims=True))
            gs_l.append(jnp.exp(b_last + m_prev - m_new))
            mnew_l.append(m_new)

        sv_l, qc_l, up_l = [], [], []
        for h in heads:
            vb = mix_ref[0, rs, cols(2, h)]
            sv_l.append(_dot(s_l[h].astype(BF16), vb))
            qc_l.append(_dot(mix_ref[0, rs, cols(0, h)], ct_ref[h].astype(BF16)))
            up_l.append(_dot(kwt_l[h], vb))
        if c + 1 < ts // ln:
            qk = scores(c + 1)

        for h in heads:
            qf = mix_ref[0, rs, cols(0, h)].astype(F32)
            num = sv_l[h] + gi_l[h] * qc_l[h]
            den = (jnp.sum(s_l[h], axis=1, keepdims=True)
                   + gi_l[h] * jnp.sum(qf * n_st[h], axis=1, keepdims=True))
            hout = num * (1.0 / jnp.maximum(jnp.abs(den), jnp.exp(-mt_l[h])))
            ct_ref[h] = gs_l[h] * ct_ref[h] + up_l[h]
            n_st[h] = gs_l[h] * n_st[h] + ksum_l[h]
            m_st[h] = mnew_l[h]
            ht = _rms(hout, hg_ref[:, cols(0, h)], dh)
            gate = mix_ref[0, rs, cols(3, h)].astype(F32)
            y_ref[0, rs, cols(0, h)] = (ht * gate).astype(BF16)

    for h in heads:
        n_ref[h] = n_st[h]
        m_ref[h] = m_st[h]


def _mlstm(mix, gates, ig_b, fg_b, h_g):
    bsz, s, _ = mix.shape
    ts = MLSTM_TS
    gbias = jnp.concatenate([ig_b, fg_b, jnp.zeros((LANES - 2 * M_HEADS,), F32)]).reshape(1, LANES)
    return pl.pallas_call(
        _mlstm_kernel,
        grid=(bsz, s // ts),
        in_specs=[
            pl.BlockSpec((1, ts, A_MIX), lambda b, t: (b, t, 0)),
            pl.BlockSpec((1, ts, LANES), lambda b, t: (b, t, 0)),
            _resident((1, LANES)),
            _resident((1, M_WIDTH)),
        ],
        out_specs=pl.BlockSpec((1, ts, M_WIDTH), lambda b, t: (b, t, 0)),
        out_shape=jax.ShapeDtypeStruct((bsz, s, M_WIDTH), BF16),
        scratch_shapes=[
            pltpu.VMEM((M_HEADS, M_HEAD_DIM, M_HEAD_DIM), F32),
            pltpu.VMEM((M_HEADS, 1, M_HEAD_DIM), F32),
            pltpu.VMEM((M_HEADS, 1, 1), F32),
        ],
        compiler_params=_params(("arbitrary", "arbitrary")),
        name="mlstm",
    )(mix, gates, gbias, h_g.reshape(1, M_WIDTH))


def _outproj_kernel(y_ref, mm_ref, mk_ref, mv_ref, qg_ref, w_ref, x_ref, o_ref):
    half = D_MODEL // 2
    heads = range(MEM_HEADS)
    hsl = lambda h: slice(h * MEM_HEAD_DIM, (h + 1) * MEM_HEAD_DIM)
    y = y_ref[0]
    acc_lo = _dot(y, w_ref[0:M_WIDTH, 0:half])
    qs = [_rms(mm_ref[0, :, hsl(h)].astype(F32), qg_ref[...], MEM_HEAD_DIM).astype(BF16) for h in heads]
    sc = [lax.dot_general(qs[h], mk_ref[0, 0, :, hsl(h)], _NT, preferred_element_type=F32) for h in heads]
    acc_hi = _dot(y, w_ref[0:M_WIDTH, half:])
    ps = [jnp.exp2(sc[h] - jnp.max(sc[h], axis=-1, keepdims=True)) for h in heads]
    ymem = []
    for h in heads:
        att = _dot(ps[h].astype(BF16), mv_ref[0, 0, :, hsl(h)]) * (1.0 / jnp.sum(ps[h], axis=-1, keepdims=True))
        z = mm_ref[0, :, MEM_WIDTH + h * MEM_HEAD_DIM:MEM_WIDTH + (h + 1) * MEM_HEAD_DIM].astype(F32)
        ymem.append((att * _silu(z)).astype(BF16))
    ymem = jnp.concatenate(ymem, axis=-1)
    o_ref[0, :, 0:half] = x_ref[0, :, 0:half] + acc_lo + _dot(ymem, w_ref[M_WIDTH:, 0:half])
    o_ref[0, :, half:] = x_ref[0, :, half:] + acc_hi + _dot(ymem, w_ref[M_WIDTH:, half:])


def _outproj(y, mm, mk_all, mv_all, layer, q_gain, w_out, w_layer, x):
    bsz, s, _ = x.shape
    nm = mk_all.shape[2]
    kv_spec = pl.BlockSpec((1, 1, nm, MEM_WIDTH), lambda b, t: (layer, b, 0, 0))
    tile = lambda w: pl.BlockSpec((1, TM, w), lambda b, t: (b, t, 0))
    qg = (q_gain * (MEM_HEAD_DIM ** -0.5 * LOG2E)).reshape(1, MEM_HEAD_DIM)
    return pl.pallas_call(
        _outproj_kernel,
        grid=(bsz, s // TM),
        in_specs=[
            tile(M_WIDTH), tile(2 * MEM_WIDTH), kv_spec, kv_spec,
            _resident((1, MEM_HEAD_DIM)),
            _layer_resident((M_WIDTH + MEM_WIDTH, D_MODEL), w_layer),
            tile(D_MODEL),
        ],
        out_specs=tile(D_MODEL),
        out_shape=jax.ShapeDtypeStruct((bsz, s, D_MODEL), F32),
        compiler_params=_params(("arbitrary", "arbitrary")),
        name="outproj",
    )(y, mm, mk_all, mv_all, qg, w_out, x)


def _rotate_half(r, cos_t, sin_t):
    half = QK_ROPE // 2
    return r * cos_t + (pltpu.roll(r, half, 1) + pltpu.roll(r, LANES - half, 1)) * sin_t


def _norm_rope_q(t, gain, cos_t, sin_t):
    nope = t[:, :QK_NOPE]
    r = t[:, QK_NOPE:]
    ss = jnp.sum(nope * nope, axis=-1, keepdims=True) + 0.5 * jnp.sum(r * r, axis=-1, keepdims=True)
    rs = lax.rsqrt(ss * (1.0 / QK_HEAD) + EPS)
    r = r * rs * gain[:, QK_NOPE:]
    return nope * rs * gain[:, :QK_NOPE], r * cos_t + pltpu.roll(r, QK_ROPE // 2, 1) * sin_t


def _inproj_b_kernel(x_ref, cos_ref, sin_ref, g_ref, win_ref, qlg_ref, wq_ref, qg_ref,
                     q_ref, zz_ref, mm_ref):
    h = _rms(x_ref[0], g_ref[...], D_MODEL).astype(BF16)
    q_lat = _dot(h, win_ref[:, 0:Q_LORA])
    zz_ref[0, :, 0:B_WIDTH // 2] = _dot(h, win_ref[:, Q_LORA:Q_LORA + B_WIDTH // 2]).astype(BF16)
    ql = _rms(q_lat, qlg_ref[...], Q_LORA).astype(BF16)
    cos_t = cos_ref[0]
    sin_t = sin_ref[0]
    heads = [_dot(ql, wq_ref[:, hd * QK_PAD:(hd + 1) * QK_PAD]) for hd in range(B_HEADS)]
    fill = 2 * LANES
    for lo in range(Q_LORA + B_WIDTH // 2, Q_LORA + B_WIDTH, fill):
        zz_ref[0, :, lo - Q_LORA:lo - Q_LORA + fill] = _dot(h, win_ref[:, lo:lo + fill]).astype(BF16)
    for lo in range(Q_LORA + B_WIDTH, win_ref.shape[1], fill):
        mo = lo - Q_LORA - B_WIDTH
        mm_ref[0, :, mo:mo + fill] = _dot(h, win_ref[:, lo:lo + fill]).astype(BF16)
    for hd in range(B_HEADS):
        nope, rot = _norm_rope_q(heads[hd], qg_ref[...], cos_t, sin_t)
        q_ref[0, hd, :, 0:QK_NOPE] = nope.astype(BF16)
        q_ref[0, hd, :, QK_NOPE:] = rot.astype(BF16)


def _dup_rope(w, heads):
    lead = w.shape[:-1]
    w = w.reshape(lead + (heads, QK_HEAD))
    w = jnp.concatenate([w, w[..., QK_NOPE:]], axis=-1)
    return w.reshape(lead + (heads * QK_PAD,))


def _rope_const():
    half = QK_ROPE // 2
    inv = ROPE_THETA ** (-jnp.arange(0, QK_ROPE, 2, dtype=F32) / QK_ROPE)
    zeros = jnp.zeros((LANES - QK_ROPE,), F32)
    ones = jnp.ones((half,), F32)
    rows = [jnp.concatenate([inv, inv, zeros]),
            jnp.concatenate([ones, ones, zeros]),
            jnp.concatenate([-ones, ones, zeros]),
            jnp.concatenate([0.0 * ones, (jnp.pi / 2) * ones, zeros])]
    return jnp.concatenate([jnp.stack(rows), jnp.zeros((SUBLANES - len(rows), LANES), F32)], axis=0)


def _inproj_b(x, cos_t, sin_t, layer, norm_g, w_in, q_lat_g, w_q_up, q_gain):
    bsz, s, _ = x.shape
    b_in = w_in.shape[-1]
    tile = lambda w: pl.BlockSpec((1, TM_B, w), lambda b, t: (b, t, 0))
    qg = _dup_rope(q_gain * (QK_HEAD ** -0.5 * LOG2E), 1).reshape(1, QK_PAD)
    return pl.pallas_call(
        _inproj_b_kernel,
        grid=(bsz, s // TM_B),
        in_specs=[
            tile(D_MODEL), tile(LANES), tile(LANES),
            _resident((1, D_MODEL)),
            _layer_resident((D_MODEL, b_in), layer),
            _resident((1, Q_LORA)),
            _layer_resident((Q_LORA, B_HEADS * QK_PAD), layer),
            _resident((1, QK_PAD)),
        ],
        out_specs=[
            pl.BlockSpec((1, B_HEADS, TM_B, QK_PAD), lambda b, t: (b, 0, t, 0)),
            tile(B_WIDTH), tile(2 * MEM_WIDTH),
        ],
        out_shape=[
            jax.ShapeDtypeStruct((bsz, B_HEADS, s, QK_PAD), BF16),
            jax.ShapeDtypeStruct((bsz, s, B_WIDTH), BF16),
            jax.ShapeDtypeStruct((bsz, s, 2 * MEM_WIDTH), BF16),
        ],
        compiler_params=_params(("arbitrary", "arbitrary")),
        name="inproj_b",
    )(x, cos_t, sin_t, norm_g.reshape(1, D_MODEL), w_in, q_lat_g.reshape(1, Q_LORA), w_q_up, qg)


def _sharedkv_kernel(x_ref, pos_ref, tab_ref, g_ref, wa_ref, lg_ref, wb_ref, kg_ref,
                     k_ref, v_ref, cos_ref, sin_ref):
    half = QK_ROPE // 2
    h = _rms(x_ref[0], g_ref[...], D_MODEL).astype(BF16)
    a = _dot(h, wa_ref[...])
    c = _rms(a[:, 0:KV_LORA], lg_ref[...], KV_LORA).astype(BF16)
    k_pe = a[:, KV_LORA:]
    ang = pos_ref[0].astype(F32) * tab_ref[0:1, :] - tab_ref[3:4, :]
    cs = jnp.cos(ang)
    low = lax.broadcasted_iota(jnp.int32, cs.shape, 1) < half
    cos_t = jnp.where(low, cs, pltpu.roll(cs, half, 1)) * tab_ref[1:2, :]
    sin_t = jnp.where(low, pltpu.roll(cs, LANES - half, 1), cs) * tab_ref[2:3, :]
    cos_ref[0] = cos_t
    sin_ref[0] = sin_t
    pe_rot = _rotate_half(k_pe * kg_ref[:, QK_NOPE:], cos_t, sin_t)
    pe_ss = jnp.sum(k_pe * k_pe, axis=-1, keepdims=True)
    for hd in range(B_HEADS):
        kv = _dot(c, wb_ref[:, hd * (QK_NOPE + V_HEAD):(hd + 1) * (QK_NOPE + V_HEAD)])
        nope = kv[:, 0:QK_NOPE]
        rs = lax.rsqrt((jnp.sum(nope * nope, axis=-1, keepdims=True) + pe_ss) * (1.0 / QK_HEAD) + EPS)
        k_ref[0, hd, :, 0:QK_NOPE] = (nope * rs * kg_ref[:, 0:QK_NOPE]).astype(BF16)
        k_ref[0, hd, :, QK_NOPE:] = (pe_rot * rs).astype(BF16)
        v_ref[0, hd] = kv[:, QK_NOPE:].astype(BF16)


def _shared_kv(x, pos3, kv_norm, w_kv_a, kv_lat_norm, w_kv_b, k_gain):
    bsz, s, _ = x.shape
    a_w = KV_LORA + LANES
    tile = lambda w: pl.BlockSpec((1, TM, w), lambda b, t: (b, t, 0))
    table = jax.ShapeDtypeStruct((bsz, s, LANES), F32)
    return pl.pallas_call(
        _sharedkv_kernel,
        grid=(bsz, s // TM),
        in_specs=[
            tile(D_MODEL), tile(1),
            _resident((SUBLANES, LANES)),
            _resident((1, D_MODEL)),
            _resident((D_MODEL, a_w)),
            _resident((1, KV_LORA)),
            _resident((KV_LORA, B_HEADS * (QK_NOPE + V_HEAD))),
            _resident((1, QK_PAD)),
        ],
        out_specs=[
            pl.BlockSpec((1, B_HEADS, TM, QK_PAD), lambda b, t: (b, 0, t, 0)),
            pl.BlockSpec((1, B_HEADS, TM, V_HEAD), lambda b, t: (b, 0, t, 0)),
            tile(LANES), tile(LANES),
        ],
        out_shape=[
            jax.ShapeDtypeStruct((bsz, B_HEADS, s, QK_PAD), BF16),
            jax.ShapeDtypeStruct((bsz, B_HEADS, s, V_HEAD), BF16),
            table, table,
        ],
        compiler_params=_params(("arbitrary", "arbitrary")),
        name="shared_kv",
    )(x, pos3, _rope_const(), kv_norm.reshape(1, D_MODEL),
      jnp.pad(w_kv_a, ((0, 0), (0, a_w - w_kv_a.shape[1]))).astype(BF16),
      kv_lat_norm.reshape(1, KV_LORA), w_kv_b.astype(BF16),
      jnp.pad(k_gain, (0, QK_PAD - QK_HEAD)).reshape(1, QK_PAD))


def _attn_kernel(q_ref, k_ref, v_ref, z_ref, o_ref):
    tq = ATTN_TQ
    nh = q_ref.shape[1]
    s = q_ref.shape[2]
    sh = CHUNK.bit_length() - 1
    rc = lax.broadcasted_iota(jnp.int32, (tq, tq), 0) >> sh
    cc = lax.broadcasted_iota(jnp.int32, (tq, tq), 1) >> sh
    dmask = cc <= rc
    nq = s // tq

    def scores(hh, qi):
        rows = slice(qi * tq, (qi + 1) * tq)
        q = q_ref[0, hh, rows, :]
        sd = lax.dot_general(q, k_ref[0, hh, rows, :], _NT, preferred_element_type=F32)
        so = None
        if qi > 0:
            so = lax.dot_general(q, k_ref[0, hh, 0:qi * tq, :], _NT, preferred_element_type=F32)
        return sd, so

    def finish(hh, qi, pd, po, l):
        rows = slice(qi * tq, (qi + 1) * tq)
        hc = slice(hh * V_HEAD, (hh + 1) * V_HEAD)
        acc = _dot(pd, v_ref[0, hh, rows, :])
        if po is not None:
            acc = acc + _dot(po, v_ref[0, hh, 0:qi * tq, :])
        o_ref[0, rows, hc] = (acc * (1.0 / l) * _silu(z_ref[0, rows, hc].astype(F32))).astype(BF16)

    nxt = [scores(hh, 0) for hh in range(nh)]
    pending = [None] * nh
    for qi in range(nq):
        for hh in range(nh):
            sd, so = nxt[hh]
            if qi + 1 < nq:
                nxt[hh] = scores(hh, qi + 1)
            if pending[hh] is not None:
                finish(*pending[hh])
            sd = jnp.where(dmask, sd, -jnp.inf)
            m = jnp.max(sd, axis=-1, keepdims=True)
            po = None
            if qi > 0:
                m = jnp.maximum(m, jnp.max(so, axis=-1, keepdims=True))
            pd = jnp.exp2(sd - m)
            l = jnp.sum(pd, axis=-1, keepdims=True)
            if qi > 0:
                po = jnp.exp2(so - m)
                l = l + jnp.sum(po, axis=-1, keepdims=True)
                po = po.astype(BF16)
            pending[hh] = (hh, qi, pd.astype(BF16), po, l)
    for hh in range(nh):
        finish(*pending[hh])


def _attention(q, k, v, zz):
    bsz, heads, s, _ = q.shape
    nh = ATTN_HEADS
    head_spec = lambda w: pl.BlockSpec((1, nh, s, w), lambda b, h: (b, h, 0, 0))
    col_spec = pl.BlockSpec((1, s, nh * V_HEAD), lambda b, h: (b, 0, h))
    return pl.pallas_call(
        _attn_kernel,
        grid=(bsz, heads // nh),
        in_specs=[head_spec(QK_PAD), head_spec(QK_PAD), head_spec(V_HEAD), col_spec],
        out_specs=col_spec,
        out_shape=jax.ShapeDtypeStruct((bsz, s, B_WIDTH), BF16),
        compiler_params=_params(("arbitrary", "arbitrary")),
        name="mla_attention",
    )(q, k, v, zz)


def kernel(x, mem, positions, a_norm, a_w_in, a_conv_w, a_conv_b, a_ig_bias, a_fg_bias, a_h_norm, a_w_out,
           b_norm, b_w_in, b_q_lat_norm, b_w_q_up, b_q_gain, b_w_out,
           kv_norm, w_kv_a, kv_lat_norm, w_kv_b, k_gain,
           mem_norm, mem_w_kv, mem_q_gain, mem_k_gain):
    bsz, s, _ = x.shape
    assert s % MLSTM_TS == 0 and s % TM == 0 and s % TM_A == 0 and s % TM_B == 0 and s % ATTN_TQ == 0
    assert MLSTM_TS % MLSTM_L == 0 and bsz % MEM_BATCH == 0
    mk_all, mv_all = _memory_kv(mem, mem_norm, mem_w_kv, mem_k_gain)
    pos3 = positions.reshape(bsz, s, 1)
    a_wmain = a_w_in.astype(BF16)
    a_wgate = jnp.pad(a_wmain[:, :, A_MAIN:A_MAIN + 2 * M_HEADS],
                      ((0, 0), (0, 0), (0, LANES - 2 * M_HEADS)))
    a_wmem = a_wmain[:, :, A_MAIN + 2 * M_HEADS:]
    a_wout = a_w_out.astype(BF16)
    b_win = b_w_in.astype(BF16)
    b_wq = _dup_rope(b_w_q_up, B_HEADS).astype(BF16)
    b_wout = b_w_out.astype(BF16)
    k_sh = v_sh = cos_t = sin_t = None
    for layer in range(DEPTH):
        if layer < N_A:
            mix, gates, mm = _inproj_a(x, layer, a_norm[layer], a_wmain, a_wgate, a_wmem,
                                       a_conv_w[layer], a_conv_b[layer])
            y = _mlstm(mix, gates, a_ig_bias[layer], a_fg_bias[layer], a_h_norm[layer])
            x = _outproj(y, mm, mk_all, mv_all, layer, mem_q_gain[layer], a_wout, layer, x)
        else:
            j = layer - N_A
            if k_sh is None:
                k_sh, v_sh, cos_t, sin_t = _shared_kv(x, pos3, kv_norm, w_kv_a, kv_lat_norm, w_kv_b, k_gain)
            q, zz, mm = _inproj_b(x, cos_t, sin_t, j, b_norm[j], b_win, b_q_lat_norm[j], b_wq, b_q_gain[j])
            y = _attention(q, k_sh, v_sh, zz)
            x = _outproj(y, mm, mk_all, mv_all, layer, mem_q_gain[layer], b_wout, j, x)
    return x
```

```python
import jax
import jax.numpy as jnp
from jax import lax
from jax.experimental import pallas as pl
from jax.experimental.pallas import tpu as pltpu

F32 = jnp.float32
BF16 = jnp.bfloat16

D_MODEL = 1024
DEPTH = 4
N_A = DEPTH // 2
CHUNK = 64
EPS = 1e-6
M_HEADS = 4
M_HEAD_DIM = 256
M_WIDTH = 1024
CONV_W = 4
B_HEADS = 8
QK_NOPE = 128
QK_ROPE = 64
QK_HEAD = 192
QK_PAD = 256
V_HEAD = 128
B_WIDTH = 1024
Q_LORA = 384
KV_LORA = 256
ROPE_THETA = 10000.0
MEM_HEADS = 4
MEM_HEAD_DIM = 128
MEM_WIDTH = 512
A_MAIN = 5 * M_WIDTH
A_MIX = 4 * M_WIDTH
LANES = 128
SUBLANES = 8
LOG2E = 1.4426950408889634

VMEM_LIMIT = 56 * 1024 * 1024

TM = 1024
TM_A = 1024
TM_B = 512
MLSTM_TS = 1024
MLSTM_L = 256
ATTN_TQ = 256
MEM_BATCH = 4
ATTN_HEADS = 4

_NT = (((1,), (1,)), ((), ()))


def _params(sem):
    return pltpu.CompilerParams(dimension_semantics=sem, vmem_limit_bytes=VMEM_LIMIT)


def _resident(shape):
    zeros = (0,) * len(shape)
    return pl.BlockSpec(shape, lambda *_: zeros, pipeline_mode=pl.Buffered(1))


def _layer_resident(shape, layer):
    index = (layer,) + (0,) * len(shape)
    return pl.BlockSpec((None,) + tuple(shape), lambda *_: index, pipeline_mode=pl.Buffered(1))


def _rms(t, g, n):
    ms = jnp.sum(t * t, axis=-1, keepdims=True) * (1.0 / n)
    return t * lax.rsqrt(ms + EPS) * g


def _silu(t):
    return t * jax.nn.sigmoid(t)


def _dot(a, b):
    return jnp.dot(a, b, preferred_element_type=F32)


def _memkv_kernel(mem_ref, g_ref, w_ref, kg_ref, mk_ref, mv_ref):
    nb, nm, _ = mem_ref.shape
    mn = _rms(mem_ref[...].reshape(nb * nm, D_MODEL), g_ref[...], D_MODEL).astype(BF16)
    for layer in range(DEPTH):
        kv = _dot(mn, w_ref[layer])
        mk = [_rms(kv[:, h * MEM_HEAD_DIM:(h + 1) * MEM_HEAD_DIM], kg_ref[layer], MEM_HEAD_DIM).astype(BF16)
              for h in range(MEM_HEADS)]
        mk_ref[layer] = jnp.concatenate(mk, axis=-1).reshape(nb, nm, MEM_WIDTH)
        mv_ref[layer] = kv[:, MEM_WIDTH:].astype(BF16).reshape(nb, nm, MEM_WIDTH)


def _memory_kv(mem, mem_norm, w_kv, k_gain):
    bsz, nm, _ = mem.shape
    nb = MEM_BATCH
    out = jax.ShapeDtypeStruct((DEPTH, bsz, nm, MEM_WIDTH), BF16)
    out_spec = pl.BlockSpec((DEPTH, nb, nm, MEM_WIDTH), lambda i: (0, i, 0, 0))
    return pl.pallas_call(
        _memkv_kernel,
        grid=(bsz // nb,),
        in_specs=[
            pl.BlockSpec((nb, nm, D_MODEL), lambda i: (i, 0, 0)),
            _resident((1, D_MODEL)),
            _resident((DEPTH, D_MODEL, 2 * MEM_WIDTH)),
            _resident((DEPTH, 1, MEM_HEAD_DIM)),
        ],
        out_specs=[out_spec, out_spec],
        out_shape=[out, out],
        compiler_params=_params(("arbitrary",)),
        name="memory_kv",
    )(mem, mem_norm.reshape(1, D_MODEL), w_kv.astype(BF16), k_gain.reshape(DEPTH, 1, MEM_HEAD_DIM))


def _inproj_a_kernel(x_ref, g_ref, wmain_ref, wgate_ref, wmem_ref, cw_ref, cb_ref,
                     mix_ref, gate_ref, mm_ref, tail_ref):
    tm = x_ref.shape[1]
    cn = 2 * LANES
    row = lax.broadcasted_iota(jnp.int32, (SUBLANES, cn), 0)

    @pl.when(pl.program_id(1) == 0)
    def _():
        tail_ref[...] = jnp.zeros_like(tail_ref)

    h = _rms(x_ref[0], g_ref[...], D_MODEL).astype(BF16)

    def conv_chunk(c):
        cs = slice(c * cn, (c + 1) * cn)
        u = _dot(h, wmain_ref[:, cs])
        u3 = jnp.concatenate([tail_ref[:, cs], u], axis=0).reshape(tm // SUBLANES + 1, SUBLANES, cn)
        acc = cb_ref[:, cs] + cw_ref[CONV_W - 1:CONV_W, cs] * u3[1:]
        for k in range(1, CONV_W):
            r = pltpu.roll(u3, k, 1)
            shifted = jnp.where(row < k, r[:-1], r[1:])
            acc = acc + cw_ref[CONV_W - 1 - k:CONV_W - k, cs] * shifted
        tail_ref[:, cs] = u[tm - SUBLANES:tm, :]
        scale = M_HEAD_DIM ** -0.5 if c * cn < M_WIDTH else 1.0
        mix_ref[0, :, cs] = (_silu(acc) * scale).reshape(tm, cn).astype(BF16)

    def value_chunk(c):
        cs = slice(2 * M_WIDTH + c * cn, 2 * M_WIDTH + (c + 1) * cn)
        mix_ref[0, :, cs] = _dot(h, wmain_ref[:, cs]).astype(BF16)

    def gate_chunk(c):
        o = _dot(h, wmain_ref[:, 3 * M_WIDTH + c * cn:3 * M_WIDTH + (c + 1) * cn])
        z = _dot(h, wmain_ref[:, 4 * M_WIDTH + c * cn:4 * M_WIDTH + (c + 1) * cn])
        mix_ref[0, :, 3 * M_WIDTH + c * cn:3 * M_WIDTH + (c + 1) * cn] = (
            jax.nn.sigmoid(o) * _silu(z)).astype(BF16)

    def mem_chunk(c):
        cs = slice(c * cn, (c + 1) * cn)
        mm_ref[0, :, cs] = _dot(h, wmem_ref[:, cs]).astype(BF16)

    for c in range(M_WIDTH // cn):
        value_chunk(c)
        conv_chunk(2 * c)
        mem_chunk(c)
        conv_chunk(2 * c + 1)
        gate_chunk(c)
    gate_ref[0] = _dot(h, wgate_ref[...])


def _inproj_a(x, layer, norm_g, wmain, wgate, wmem, conv_w, conv_b):
    bsz, s, _ = x.shape
    tm = TM_A
    tile = lambda w: pl.BlockSpec((1, tm, w), lambda b, t: (b, t, 0))
    return pl.pallas_call(
        _inproj_a_kernel,
        grid=(bsz, s // tm),
        in_specs=[
            tile(D_MODEL),
            _resident((1, D_MODEL)),
            _layer_resident((D_MODEL, A_MAIN), layer),
            _layer_resident((D_MODEL, LANES), layer),
            _layer_resident((D_MODEL, 2 * MEM_WIDTH), layer),
            _resident((CONV_W, 2 * M_WIDTH)),
            _resident((1, 2 * M_WIDTH)),
        ],
        out_specs=[tile(A_MIX), tile(LANES), tile(2 * MEM_WIDTH)],
        out_shape=[
            jax.ShapeDtypeStruct((bsz, s, A_MIX), BF16),
            jax.ShapeDtypeStruct((bsz, s, LANES), F32),
            jax.ShapeDtypeStruct((bsz, s, 2 * MEM_WIDTH), BF16),
        ],
        scratch_shapes=[pltpu.VMEM((SUBLANES, 2 * M_WIDTH), F32)],
        compiler_params=_params(("arbitrary", "arbitrary")),
        name="inproj_a",
    )(x, norm_g.reshape(1, D_MODEL), wmain, wgate, wmem, conv_w, conv_b.reshape(1, 2 * M_WIDTH))


def _mlstm_kernel(mix_ref, gate_ref, gbias_ref, hg_ref, y_ref, ct_ref, n_ref, m_ref):
    ts, ln, dh = MLSTM_TS, MLSTM_L, M_HEAD_DIM
    heads = range(M_HEADS)

    @pl.when(pl.program_id(1) == 0)
    def _():
        ct_ref[...] = jnp.zeros_like(ct_ref)
        n_ref[...] = jnp.zeros_like(n_ref)
        m_ref[...] = jnp.zeros_like(m_ref)

    def cols(group, h):
        return slice(group * M_WIDTH + h * dh, group * M_WIDTH + (h + 1) * dh)

    def scores(c):
        rs = slice(c * ln, (c + 1) * ln)
        return [lax.dot_general(mix_ref[0, rs, cols(0, h)], mix_ref[0, rs, cols(1, h)], _NT,
                                preferred_element_type=F32) for h in heads]

    qk = scores(0)

    g = gate_ref[0] + gbias_ref[...]
    lsg = jnp.minimum(g, 0.0) - jnp.log1p(jnp.exp(-jnp.abs(g)))
    tri = (lax.broadcasted_iota(jnp.int32, (ln, ln), 0)
           >= lax.broadcasted_iota(jnp.int32, (ln, ln), 1))
    tri_b = jnp.where(tri, 1.0, 0.0).astype(BF16)
    p0 = lsg.astype(BF16)
    r0 = lsg - p0.astype(F32)
    p1 = r0.astype(BF16)
    p2 = (r0 - p1.astype(F32)).astype(BF16)
    pieces = jnp.concatenate([p0, p1, p2], axis=1)
    bcum = []
    for c in range(ts // ln):
        part = _dot(tri_b, pieces[c * ln:(c + 1) * ln])
        bcum.append(part[:, 0:LANES] + part[:, LANES:2 * LANES] + part[:, 2 * LANES:])
    bcum = jnp.concatenate(bcum, axis=0)
    g_t = g.T
    b_t = bcum.T

    m_st = [m_ref[h] for h in heads]
    n_st = [n_ref[h] for h in heads]

    for c in range(ts // ln):
        rs = slice(c * ln, (c + 1) * ln)
        s_l, gi_l, mt_l, kwt_l, ksum_l, gs_l, mnew_l = [], [], [], [], [], [], []
        for h in heads:
            b_col = bcum[rs, M_HEADS + h:M_HEADS + h + 1]
            b_row = b_t[M_HEADS + h:M_HEADS + h + 1, rs]
            i_col = g[rs, h:h + 1]
            i_row = g_t[h:h + 1, rs]
            m_prev = m_st[h]
            d = jnp.where(tri, b_col - b_row + i_row, -jnp.inf)
            inter = b_col + m_prev
            m_t = jnp.maximum(inter, jnp.max(d, axis=1, keepdims=True))
            s_l.append(qk[h] * jnp.exp(d - m_t))
            gi_l.append(jnp.exp(inter - m_t))
            mt_l.append(m_t)
            b_last = b_col[ln - 1:ln, :]
            a_col = b_last - b_col + i_col
            m_new = jnp.maximum(b_last + m_prev, jnp.max(a_col, axis=0, keepdims=True))
            kw = mix_ref[0, rs, cols(1, h)].astype(F32) * jnp.exp(a_col - m_new)
            kwt_l.append(kw.T.astype(BF16))
            ksum_l.append(jnp.sum(kw, axis=0, keepdims=True))
            gs_l.append(jnp.exp(b_last + m_prev - m_new))
            mnew_l.append(m_new)

        sv_l, qc_l, up_l = [], [], []
        for h in heads:
            vb = mix_ref[0, rs, cols(2, h)]
            sv_l.append(_dot(s_l[h].astype(BF16), vb))
            qc_l.append(_dot(mix_ref[0, rs, cols(0, h)], ct_ref[h].astype(BF16)))
            up_l.append(_dot(kwt_l[h], vb))
        if c + 1 < ts // ln:
            qk = scores(c + 1)

        for h in heads:
            qf = mix_ref[0, rs, cols(0, h)].astype(F32)
            num = sv_l[h] + gi_l[h] * qc_l[h]
            den = (jnp.sum(s_l[h], axis=1, keepdims=True)
                   + gi_l[h] * jnp.sum(qf * n_st[h], axis=1, keepdims=True))
            hout = num * (1.0 / jnp.maximum(jnp.abs(den), jnp.exp(-mt_l[h])))
            ct_ref[h] = gs_l[h] * ct_ref[h] + up_l[h]
            n_st[h] = gs_l[h] * n_st[h] + ksum_l[h]
            m_st[h] = mnew_l[h]
            ht = _rms(hout, hg_ref[:, cols(0, h)], dh)
            gate = mix_ref[0, rs, cols(3, h)].astype(F32)
            y_ref[0, rs, cols(0, h)] = (ht * gate).astype(BF16)

    for h in heads:
        n_ref[h] = n_st[h]
        m_ref[h] = m_st[h]


def _mlstm(mix, gates, ig_b, fg_b, h_g):
    bsz, s, _ = mix.shape
    ts = MLSTM_TS
    gbias = jnp.concatenate([ig_b, fg_b, jnp.zeros((LANES - 2 * M_HEADS,), F32)]).reshape(1, LANES)
    return pl.pallas_call(
        _mlstm_kernel,
        grid=(bsz, s // ts),
        in_specs=[
            pl.BlockSpec((1, ts, A_MIX), lambda b, t: (b, t, 0)),
            pl.BlockSpec((1, ts, LANES), lambda b, t: (b, t, 0)),
            _resident((1, LANES)),
            _resident((1, M_WIDTH)),
        ],
        out_specs=pl.BlockSpec((1, ts, M_WIDTH), lambda b, t: (b, t, 0)),
        out_shape=jax.ShapeDtypeStruct((bsz, s, M_WIDTH), BF16),
        scratch_shapes=[
            pltpu.VMEM((M_HEADS, M_HEAD_DIM, M_HEAD_DIM), F32),
            pltpu.VMEM((M_HEADS, 1, M_HEAD_DIM), F32),
            pltpu.VMEM((M_HEADS, 1, 1), F32),
        ],
        compiler_params=_params(("arbitrary", "arbitrary")),
        name="mlstm",
    )(mix, gates, gbias, h_g.reshape(1, M_WIDTH))


def _outproj_kernel(y_ref, mm_ref, mk_ref, mv_ref, qg_ref, w_ref, x_ref, o_ref):
    half = D_MODEL // 2
    heads = range(MEM_HEADS)
    hsl = lambda h: slice(h * MEM_HEAD_DIM, (h + 1) * MEM_HEAD_DIM)
    y = y_ref[0]
    acc_lo = _dot(y, w_ref[0:M_WIDTH, 0:half])
    qs = [_rms(mm_ref[0, :, hsl(h)].astype(F32), qg_ref[...], MEM_HEAD_DIM).astype(BF16) for h in heads]
    sc = [lax.dot_general(qs[h], mk_ref[0, 0, :, hsl(h)], _NT, preferred_element_type=F32) for h in heads]
    acc_hi = _dot(y, w_ref[0:M_WIDTH, half:])
    ps = [jnp.exp2(sc[h] - jnp.max(sc[h], axis=-1, keepdims=True)) for h in heads]
    ymem = []
    for h in heads:
        att = _dot(ps[h].astype(BF16), mv_ref[0, 0, :, hsl(h)]) * (1.0 / jnp.sum(ps[h], axis=-1, keepdims=True))
        z = mm_ref[0, :, MEM_WIDTH + h * MEM_HEAD_DIM:MEM_WIDTH + (h + 1) * MEM_HEAD_DIM].astype(F32)
        ymem.append((att * _silu(z)).astype(BF16))
    ymem = jnp.concatenate(ymem, axis=-1)
    o_ref[0, :, 0:half] = x_ref[0, :, 0:half] + acc_lo + _dot(ymem, w_ref[M_WIDTH:, 0:half])
    o_ref[0, :, half:] = x_ref[0, :, half:] + acc_hi + _dot(ymem, w_ref[M_WIDTH:, half:])


def _outproj(y, mm, mk_all, mv_all, layer, q_gain, w_out, w_layer, x):
    bsz, s, _ = x.shape
    nm = mk_all.shape[2]
    kv_spec = pl.BlockSpec((1, 1, nm, MEM_WIDTH), lambda b, t: (layer, b, 0, 0))
    tile = lambda w: pl.BlockSpec((1, TM, w), lambda b, t: (b, t, 0))
    qg = (q_gain * (MEM_HEAD_DIM ** -0.5 * LOG2E)).reshape(1, MEM_HEAD_DIM)
    return pl.pallas_call(
        _outproj_kernel,
        grid=(bsz, s // TM),
        in_specs=[
            tile(M_WIDTH), tile(2 * MEM_WIDTH), kv_spec, kv_spec,
            _resident((1, MEM_HEAD_DIM)),
            _layer_resident((M_WIDTH + MEM_WIDTH, D_MODEL), w_layer),
            tile(D_MODEL),
        ],
        out_specs=tile(D_MODEL),
        out_shape=jax.ShapeDtypeStruct((bsz, s, D_MODEL), F32),
        compiler_params=_params(("arbitrary", "arbitrary")),
        name="outproj",
    )(y, mm, mk_all, mv_all, qg, w_out, x)


def _rotate_half(r, cos_t, sin_t):
    half = QK_ROPE // 2
    return r * cos_t + (pltpu.roll(r, half, 1) + pltpu.roll(r, LANES - half, 1)) * sin_t


def _norm_rope_q(t, gain, cos_t, sin_t):
    nope = t[:, :QK_NOPE]
    r = t[:, QK_NOPE:]
    ss = jnp.sum(nope * nope, axis=-1, keepdims=True) + 0.5 * jnp.sum(r * r, axis=-1, keepdims=True)
    rs = lax.rsqrt(ss * (1.0 / QK_HEAD) + EPS)
    r = r * rs * gain[:, QK_NOPE:]
    return nope * rs * gain[:, :QK_NOPE], r * cos_t + pltpu.roll(r, QK_ROPE // 2, 1) * sin_t


def _inproj_b_kernel(x_ref, cos_ref, sin_ref, g_ref, win_ref, qlg_ref, wq_ref, qg_ref,
                     q_ref, zz_ref, mm_ref):
    h = _rms(x_ref[0], g_ref[...], D_MODEL).astype(BF16)
    q_lat = _dot(h, win_ref[:, 0:Q_LORA])
    zz_ref[0, :, 0:B_WIDTH // 2] = _dot(h, win_ref[:, Q_LORA:Q_LORA + B_WIDTH // 2]).astype(BF16)
    ql = _rms(q_lat, qlg_ref[...], Q_LORA).astype(BF16)
    cos_t = cos_ref[0]
    sin_t = sin_ref[0]
    heads = [_dot(ql, wq_ref[:, hd * QK_PAD:(hd + 1) * QK_PAD]) for hd in range(B_HEADS)]
    fill = 2 * LANES
    for lo in range(Q_LORA + B_WIDTH // 2, Q_LORA + B_WIDTH, fill):
        zz_ref[0, :, lo - Q_LORA:lo - Q_LORA + fill] = _dot(h, win_ref[:, lo:lo + fill]).astype(BF16)
    for lo in range(Q_LORA + B_WIDTH, win_ref.shape[1], fill):
        mo = lo - Q_LORA - B_WIDTH
        mm_ref[0, :, mo:mo + fill] = _dot(h, win_ref[:, lo:lo + fill]).astype(BF16)
    for hd in range(B_HEADS):
        nope, rot = _norm_rope_q(heads[hd], qg_ref[...], cos_t, sin_t)
        q_ref[0, hd, :, 0:QK_NOPE] = nope.astype(BF16)
        q_ref[0, hd, :, QK_NOPE:] = rot.astype(BF16)


def _dup_rope(w, heads):
    lead = w.shape[:-1]
    w = w.reshape(lead + (heads, QK_HEAD))
    w = jnp.concatenate([w, w[..., QK_NOPE:]], axis=-1)
    return w.reshape(lead + (heads * QK_PAD,))


def _rope_const():
    half = QK_ROPE // 2
    inv = ROPE_THETA ** (-jnp.arange(0, QK_ROPE, 2, dtype=F32) / QK_ROPE)
    zeros = jnp.zeros((LANES - QK_ROPE,), F32)
    ones = jnp.ones((half,), F32)
    rows = [jnp.concatenate([inv, inv, zeros]),
            jnp.concatenate([ones, ones, zeros]),
            jnp.concatenate([-ones, ones, zeros]),
            jnp.concatenate([0.0 * ones, (jnp.pi / 2) * ones, zeros])]
    return jnp.concatenate([jnp.stack(rows), jnp.zeros((SUBLANES - len(rows), LANES), F32)], axis=0)


def _inproj_b(x, cos_t, sin_t, layer, norm_g, w_in, q_lat_g, w_q_up, q_gain):
    bsz, s, _ = x.shape
    b_in = w_in.shape[-1]
    tile = lambda w: pl.BlockSpec((1, TM_B, w), lambda b, t: (b, t, 0))
    qg = _dup_rope(q_gain * (QK_HEAD ** -0.5 * LOG2E), 1).reshape(1, QK_PAD)
    return pl.pallas_call(
        _inproj_b_kernel,
        grid=(bsz, s // TM_B),
        in_specs=[
            tile(D_MODEL), tile(LANES), tile(LANES),
            _resident((1, D_MODEL)),
            _layer_resident((D_MODEL, b_in), layer),
            _resident((1, Q_LORA)),
            _layer_resident((Q_LORA, B_HEADS * QK_PAD), layer),
            _resident((1, QK_PAD)),
        ],
        out_specs=[
            pl.BlockSpec((1, B_HEADS, TM_B, QK_PAD), lambda b, t: (b, 0, t, 0)),
            tile(B_WIDTH), tile(2 * MEM_WIDTH),
        ],
        out_shape=[
            jax.ShapeDtypeStruct((bsz, B_HEADS, s, QK_PAD), BF16),
            jax.ShapeDtypeStruct((bsz, s, B_WIDTH), BF16),
            jax.ShapeDtypeStruct((bsz, s, 2 * MEM_WIDTH), BF16),
        ],
        compiler_params=_params(("arbitrary", "arbitrary")),
        name="inproj_b",
    )(x, cos_t, sin_t, norm_g.reshape(1, D_MODEL), w_in, q_lat_g.reshape(1, Q_LORA), w_q_up, qg)


def _sharedkv_kernel(x_ref, pos_ref, tab_ref, g_ref, wa_ref, lg_ref, wb_ref, kg_ref,
                     k_ref, v_ref, cos_ref, sin_ref):
    half = QK_ROPE // 2
    h = _rms(x_ref[0], g_ref[...], D_MODEL).astype(BF16)
    a = _dot(h, wa_ref[...])
    c = _rms(a[:, 0:KV_LORA], lg_ref[...], KV_LORA).astype(BF16)
    k_pe = a[:, KV_LORA:]
    ang = pos_ref[0].astype(F32) * tab_ref[0:1, :] - tab_ref[3:4, :]
    cs = jnp.cos(ang)
    low = lax.broadcasted_iota(jnp.int32, cs.shape, 1) < half
    cos_t = jnp.where(low, cs, pltpu.roll(cs, half, 1)) * tab_ref[1:2, :]
    sin_t = jnp.where(low, pltpu.roll(cs, LANES - half, 1), cs) * tab_ref[2:3, :]
    cos_ref[0] = cos_t
    sin_ref[0] = sin_t
    pe_rot = _rotate_half(k_pe * kg_ref[:, QK_NOPE:], cos_t, sin_t)
    pe_ss = jnp.sum(k_pe * k_pe, axis=-1, keepdims=True)
    for hd in range(B_HEADS):
        kv = _dot(c, wb_ref[:, hd * (QK_NOPE + V_HEAD):(hd + 1) * (QK_NOPE + V_HEAD)])
        nope = kv[:, 0:QK_NOPE]
        rs = lax.rsqrt((jnp.sum(nope * nope, axis=-1, keepdims=True) + pe_ss) * (1.0 / QK_HEAD) + EPS)
        k_ref[0, hd, :, 0:QK_NOPE] = (nope * rs * kg_ref[:, 0:QK_NOPE]).astype(BF16)
        k_ref[0, hd, :, QK_NOPE:] = (pe_rot * rs).astype(BF16)
        v_ref[0, hd] = kv[:, QK_NOPE:].astype(BF16)


def _shared_kv(x, pos3, kv_norm, w_kv_a, kv_lat_norm, w_kv_b, k_gain):
    bsz, s, _ = x.shape
    a_w = KV_LORA + LANES
    tile = lambda w: pl.BlockSpec((1, TM, w), lambda b, t: (b, t, 0))
    table = jax.ShapeDtypeStruct((bsz, s, LANES), F32)
    return pl.pallas_call(
        _sharedkv_kernel,
        grid=(bsz, s // TM),
        in_specs=[
            tile(D_MODEL), tile(1),
            _resident((SUBLANES, LANES)),
            _resident((1, D_MODEL)),
            _resident((D_MODEL, a_w)),
            _resident((1, KV_LORA)),
            _resident((KV_LORA, B_HEADS * (QK_NOPE + V_HEAD))),
            _resident((1, QK_PAD)),
        ],
        out_specs=[
            pl.BlockSpec((1, B_HEADS, TM, QK_PAD), lambda b, t: (b, 0, t, 0)),
            pl.BlockSpec((1, B_HEADS, TM, V_HEAD), lambda b, t: (b, 0, t, 0)),
            tile(LANES), tile(LANES),
        ],
        out_shape=[
            jax.ShapeDtypeStruct((bsz, B_HEADS, s, QK_PAD), BF16),
            jax.ShapeDtypeStruct((bsz, B_HEADS, s, V_HEAD), BF16),
            table, table,
        ],
        compiler_params=_params(("arbitrary", "arbitrary")),
        name="shared_kv",
    )(x, pos3, _rope_const(), kv_norm.reshape(1, D_MODEL),
      jnp.pad(w_kv_a, ((0, 0), (0, a_w - w_kv_a.shape[1]))).astype(BF16),
      kv_lat_norm.reshape(1, KV_LORA), w_kv_b.astype(BF16),
      jnp.pad(k_gain, (0, QK_PAD - QK_HEAD)).reshape(1, QK_PAD))


def _attn_kernel(q_ref, k_ref, v_ref, z_ref, o_ref):
    tq = ATTN_TQ
    nh = q_ref.shape[1]
    s = q_ref.shape[2]
    sh = CHUNK.bit_length() - 1
    rc = lax.broadcasted_iota(jnp.int32, (tq, tq), 0) >> sh
    cc = lax.broadcasted_iota(jnp.int32, (tq, tq), 1) >> sh
    dmask = cc <= rc
    nq = s // tq

    def scores(hh, qi):
        rows = slice(qi * tq, (qi + 1) * tq)
        q = q_ref[0, hh, rows, :]
        sd = lax.dot_general(q, k_ref[0, hh, rows, :], _NT, preferred_element_type=F32)
        so = None
        if qi > 0:
            so = lax.dot_general(q, k_ref[0, hh, 0:qi * tq, :], _NT, preferred_element_type=F32)
        return sd, so

    def finish(hh, qi, pd, po, l):
        rows = slice(qi * tq, (qi + 1) * tq)
        hc = slice(hh * V_HEAD, (hh + 1) * V_HEAD)
        acc = _dot(pd, v_ref[0, hh, rows, :])
        if po is not None:
            acc = acc + _dot(po, v_ref[0, hh, 0:qi * tq, :])
        o_ref[0, rows, hc] = (acc * (1.0 / l) * _silu(z_ref[0, rows, hc].astype(F32))).astype(BF16)

    nxt = [scores(hh, 0) for hh in range(nh)]
    pending = [None] * nh
    for qi in range(nq):
        for hh in range(nh):
            sd, so = nxt[hh]
            if qi + 1 < nq:
                nxt[hh] = scores(hh, qi + 1)
            if pending[hh] is not None:
                finish(*pending[hh])
            sd = jnp.where(dmask, sd, -jnp.inf)
            m = jnp.max(sd, axis=-1, keepdims=True)
            po = None
            if qi > 0:
                m = jnp.maximum(m, jnp.max(so, axis=-1, keepdims=True))
            pd = jnp.exp2(sd - m)
            l = jnp.sum(pd, axis=-1, keepdims=True)
            if qi > 0:
                po = jnp.exp2(so - m)
                l = l + jnp.sum(po, axis=-1, keepdims=True)
                po = po.astype(BF16)
            pending[hh] = (hh, qi, pd.astype(BF16), po, l)
    for hh in range(nh):
        finish(*pending[hh])


def _attention(q, k, v, zz):
    bsz, heads, s, _ = q.shape
    nh = ATTN_HEADS
    head_spec = lambda w: pl.BlockSpec((1, nh, s, w), lambda b, h: (b, h, 0, 0))
    col_spec = pl.BlockSpec((1, s, nh * V_HEAD), lambda b, h: (b, 0, h))
    return pl.pallas_call(
        _attn_kernel,
        grid=(bsz, heads // nh),
        in_specs=[head_spec(QK_PAD), head_spec(QK_PAD), head_spec(V_HEAD), col_spec],
        out_specs=col_spec,
        out_shape=jax.ShapeDtypeStruct((bsz, s, B_WIDTH), BF16),
        compiler_params=_params(("arbitrary", "arbitrary")),
        name="mla_attention",
    )(q, k, v, zz)


def kernel(x, mem, positions, a_norm, a_w_in, a_conv_w, a_conv_b, a_ig_bias, a_fg_bias, a_h_norm, a_w_out,
           b_norm, b_w_in, b_q_lat_norm, b_w_q_up, b_q_gain, b_w_out,
           kv_norm, w_kv_a, kv_lat_norm, w_kv_b, k_gain,
           mem_norm, mem_w_kv, mem_q_gain, mem_k_gain):
    bsz, s, _ = x.shape
    assert s % MLSTM_TS == 0 and s % TM == 0 and s % TM_A == 0 and s % TM_B == 0 and s % ATTN_TQ == 0
    assert MLSTM_TS % MLSTM_L == 0 and bsz % MEM_BATCH == 0
    mk_all, mv_all = _memory_kv(mem, mem_norm, mem_w_kv, mem_k_gain)
    pos3 = positions.reshape(bsz, s, 1)
    a_wmain = a_w_in.astype(BF16)
    a_wgate = jnp.pad(a_wmain[:, :, A_MAIN:A_MAIN + 2 * M_HEADS],
                      ((0, 0), (0, 0), (0, LANES - 2 * M_HEADS)))
    a_wmem = a_wmain[:, :, A_MAIN + 2 * M_HEADS:]
    a_wout = a_w_out.astype(BF16)
    b_win = b_w_in.astype(BF16)
    b_wq = _dup_rope(b_w_q_up, B_HEADS).astype(BF16)
    b_wout = b_w_out.astype(BF16)
    k_sh = v_sh = cos_t = sin_t = None
    for layer in range(DEPTH):
        if layer < N_A:
            mix, gates, mm = _inproj_a(x, layer, a_norm[layer], a_wmain, a_wgate, a_wmem,
                                       a_conv_w[layer], a_conv_b[layer])
            y = _mlstm(mix, gates, a_ig_bias[layer], a_fg_bias[layer], a_h_norm[layer])
            x = _outproj(y, mm, mk_all, mv_all, layer, mem_q_gain[layer], a_wout, layer, x)
        else:
            j = layer - N_A
            if k_sh is None:
                k_sh, v_sh, cos_t, sin_t = _shared_kv(x, pos3, kv_norm, w_kv_a, kv_lat_norm, w_kv_b, k_gain)
            q, zz, mm = _inproj_b(x, cos_t, sin_t, j, b_norm[j], b_win, b_q_lat_norm[j], b_wq, b_q_gain[j])
            y = _attention(q, k_sh, v_sh, zz)
            x = _outproj(y, mm, mk_all, mv_all, layer, mem_q_gain[layer], b_wout, j, x)
    return x
```
